```python
import jax
import jax.numpy as jnp
from jax import lax
import numpy as np

D_MODEL = 1024
BATCH = 4
SEQ = 4096
DEPTH = 2
DEC_BATCH = 32
DEC_SEQ = 8
PAST_LEN = 8192
PAGE_SIZE = 128

HEAD_DIM = 64
CONV_CH = D_MODEL // 2
CONV_WIDTH = 31
FOX_HEADS = (D_MODEL // 2) // HEAD_DIM
FOX_WIDTH = FOX_HEADS * HEAD_DIM
SB_HEADS = D_MODEL // HEAD_DIM
SB_WIDTH = SB_HEADS * HEAD_DIM
N_MEM = 256
MEM_HEADS = 4
MEM_HEAD_DIM = 128
MEM_WIDTH = MEM_HEADS * MEM_HEAD_DIM
D_FF = 2816
FFN_CONV_WIDTH = 3
Q_BLOCK = 128
N_EVEN = (DEPTH + 1) // 2
N_ODD = DEPTH // 2
EVEN_SPLITS = [CONV_CH, 2 * CONV_CH, 2 * CONV_CH + FOX_WIDTH,
               2 * CONV_CH + 2 * FOX_WIDTH, 2 * CONV_CH + 3 * FOX_WIDTH]
EVEN_IN = 2 * CONV_CH + 3 * FOX_WIDTH + FOX_HEADS
N_NORMS = 6
FORGET_BIAS_INIT = 2.0
EPS = 1e-6
NEG_INF = -1e30

kernel_name = 'hybrid_conformer_fox_stickbreak_decoder_step'


def rmsnorm(x, g):
    x32 = x.astype(jnp.float32)
    y = x32 * lax.rsqrt(jnp.mean(x32 * x32, axis=-1, keepdims=True) + EPS)
    return (y * g.astype(jnp.float32)).astype(x.dtype)


def layernorm(x, g, b):
    x32 = x.astype(jnp.float32)
    mu = jnp.mean(x32, axis=-1, keepdims=True)
    var = jnp.mean(jnp.square(x32 - mu), axis=-1, keepdims=True)
    y = (x32 - mu) * lax.rsqrt(var + EPS) * g.astype(jnp.float32) + b.astype(jnp.float32)
    return y.astype(x.dtype)


def causal_dwconv(buf, x, w, b):
    width, ch = w.shape
    full = jnp.concatenate([buf.astype(x.dtype), x], axis=1)
    y = lax.conv_general_dilated(full, w[:, None, :].astype(x.dtype), window_strides=(1,),
                                 padding='VALID', dimension_numbers=('NWC', 'WIO', 'NWC'),
                                 feature_group_count=ch)
    return y + b.astype(x.dtype), full[:, full.shape[1] - (width - 1):]


def gather_pages(pool, page_table):
    rows = pool[page_table]
    return rows.reshape(page_table.shape[0], page_table.shape[1] * pool.shape[1], *pool.shape[2:])


def sweep_query_blocks(block_fn, n_q):
    size = min(Q_BLOCK, n_q)
    return jnp.concatenate([block_fn(s, min(s + size, n_q)) for s in range(0, n_q, size)], axis=1)


def forgetting_attention(q, k, v, cum_logf):
    n_q, n_k = q.shape[1], k.shape[1]
    offset = n_k - n_q
    scale = HEAD_DIM ** -0.5
    c = jnp.swapaxes(cum_logf, 1, 2)

    def block(start, stop):
        kv_len = offset + stop
        t_pos = offset + jnp.arange(start, stop)
        s_pos = jnp.arange(kv_len)
        logits = jnp.einsum('bqhd,bkhd->bhqk', q[:, start:stop], k[:, :kv_len]).astype(jnp.float32) * scale
        logits = logits + c[:, :, offset + start:offset + stop, None] - c[:, :, None, :kv_len]
        logits = jnp.where(s_pos[None, :] <= t_pos[:, None], logits, NEG_INF)
        p = jax.nn.softmax(logits, axis=-1)
        return jnp.einsum('bhqk,bkhd->bqhd', p.astype(v.dtype), v[:, :kv_len])

    return sweep_query_blocks(block, n_q)


def stick_breaking_attention(q, k, v):
    n_q, n_k = q.shape[1], k.shape[1]
    offset = n_k - n_q
    scale = HEAD_DIM ** -0.5

    def block(start, stop):
        kv_len = offset + stop
        t_pos = offset + jnp.arange(start, stop)
        s_pos = jnp.arange(kv_len)
        earlier = s_pos[None, :] < t_pos[:, None]
        z = jnp.einsum('bqhd,bkhd->bhqk', q[:, start:stop], k[:, :kv_len]).astype(jnp.float32) * scale
        log_remain = jnp.where(earlier, jax.nn.log_sigmoid(-z), 0.0)
        log_stick = lax.cumsum(log_remain, axis=3, reverse=True) - log_remain
        weights = jnp.where(earlier, jnp.exp(jax.nn.log_sigmoid(z) + log_stick), 0.0)
        return jnp.einsum('bhqk,bkhd->bqhd', weights.astype(v.dtype), v[:, :kv_len])

    return sweep_query_blocks(block, n_q)


def even_mixer(h, conv_buf, past, w_in, b_f, conv_w, conv_b, ln_g, ln_b, w_out):
    bsz, n, _ = h.shape
    a, gate, q, k, v, f = jnp.split(h @ w_in, EVEN_SPLITS, axis=-1)
    u = a * jax.nn.sigmoid(gate)
    u_conv, new_buf = causal_dwconv(conv_buf, u, conv_w, conv_b)
    a_out = jax.nn.silu(layernorm(u_conv, ln_g, ln_b))
    q = q.reshape(bsz, n, FOX_HEADS, HEAD_DIM)
    k = k.reshape(bsz, n, FOX_HEADS, HEAD_DIM)
    v = v.reshape(bsz, n, FOX_HEADS, HEAD_DIM)
    logf = jax.nn.log_sigmoid(f.astype(jnp.float32) + b_f.astype(jnp.float32))
    if past is None:
        k_all, v_all, logf_all = k, v, logf
    else:
        k_all = jnp.concatenate([past[0].astype(k.dtype), k], axis=1)
        v_all = jnp.concatenate([past[1].astype(v.dtype), v], axis=1)
        logf_all = jnp.concatenate([past[2].astype(jnp.float32), logf], axis=1)
    o = forgetting_attention(q, k_all, v_all, jnp.cumsum(logf_all, axis=1))
    merged = jnp.concatenate([a_out, o.reshape(bsz, n, FOX_WIDTH).astype(a_out.dtype)], axis=-1)
    return merged @ w_out, new_buf, k, v, logf


def odd_mixer(h, past, w_in, w_out):
    bsz, n, _ = h.shape
    q, k, v = jnp.split(h @ w_in, 3, axis=-1)
    q = q.reshape(bsz, n, SB_HEADS, HEAD_DIM)
    k = k.reshape(bsz, n, SB_HEADS, HEAD_DIM)
    v = v.reshape(bsz, n, SB_HEADS, HEAD_DIM)
    if past is None:
        k_all, v_all = k, v
    else:
        k_all = jnp.concatenate([past[0].astype(k.dtype), k], axis=1)
        v_all = jnp.concatenate([past[1].astype(v.dtype), v], axis=1)
    o = stick_breaking_attention(q, k_all, v_all)
    return o.reshape(bsz, n, SB_WIDTH) @ w_out, k, v


def memory_kv(mem, g, wk, wv):
    m = rmsnorm(mem, g)
    bsz, n_mem, _ = mem.shape
    return ((m @ wk).reshape(bsz, n_mem, MEM_HEADS, MEM_HEAD_DIM),
            (m @ wv).reshape(bsz, n_mem, MEM_HEADS, MEM_HEAD_DIM))


def memory_attention(h, mk, mv, wq, wo):
    bsz, n, _ = h.shape
    q = (h @ wq).reshape(bsz, n, MEM_HEADS, MEM_HEAD_DIM)
    logits = jnp.einsum('bqhd,bmhd->bhqm', q, mk.astype(q.dtype)).astype(jnp.float32) * MEM_HEAD_DIM ** -0.5
    p = jax.nn.softmax(logits, axis=-1)
    o = jnp.einsum('bhqm,bmhd->bqhd', p.astype(h.dtype), mv.astype(h.dtype))
    return o.reshape(bsz, n, MEM_WIDTH) @ wo


def conv_ffn(h, buf, w_up, conv_w, conv_b, w_down):
    u, new_buf = causal_dwconv(buf, h @ w_up, conv_w, conv_b)
    gate, val = jnp.split(u, 2, axis=-1)
    return (jax.nn.silu(gate) * val) @ w_down, new_buf


def run_trunk(x, mem_k, mem_v, conv_state, ffn_state, fox_cache, sb_cache, page_table, w):
    fox_k, fox_v, fox_logf, conv_new, sb_k, sb_v, ffn_new = [], [], [], [], [], [], []
    for layer in range(DEPTH):
        g = w['norm_g'][layer]
        i = layer // 2
        h = rmsnorm(x, g[0])
        if layer % 2 == 0:
            past = None if fox_cache is None else tuple(gather_pages(c[i], page_table) for c in fox_cache)
            y, buf, k, v, logf = even_mixer(h, conv_state[i], past, w['even_w_in'][i], w['even_b_f'][i],
                                            w['conv_w'][i], w['conv_b'][i], w['conv_ln_g'][i],
                                            w['conv_ln_b'][i], w['even_w_out'][i])
            fox_k.append(k)
            fox_v.append(v)
            fox_logf.append(logf)
            conv_new.append(buf)
        else:
            past = None if sb_cache is None else tuple(gather_pages(c[i], page_table) for c in sb_cache)
            y, k, v = odd_mixer(h, past, w['sb_w_in'][i], w['sb_w_out'][i])
            sb_k.append(k)
            sb_v.append(v)
        x = x + rmsnorm(y, g[1])
        h = rmsnorm(x, g[2])
        x = x + rmsnorm(memory_attention(h, mem_k[layer], mem_v[layer], w['mem_wq'][layer], w['mem_wo'][layer]), g[3])
        h = rmsnorm(x, g[4])
        y, buf = conv_ffn(h, ffn_state[layer], w['ffn_w_up'][layer], w['ffn_conv_w'][layer],
                          w['ffn_conv_b'][layer], w['ffn_w_down'][layer])
        ffn_new.append(buf)
        x = x + rmsnorm(y, g[5])
    return (x, jnp.stack(fox_k), jnp.stack(fox_v), jnp.stack(fox_logf), jnp.stack(conv_new),
            jnp.stack(sb_k), jnp.stack(sb_v), jnp.stack(ffn_new))


def setup_inputs(seed: int = 0) -> dict:
    key = jax.random.key(seed)
    keys = jax.random.split(key, 32)
    n_pages = PAST_LEN // PAGE_SIZE
    n_pool = (5 * DEC_BATCH * n_pages + 3) // 4

    def normal(i, shape, scale=1.0):
        return scale * jax.random.normal(keys[i], shape, jnp.float32)

    page_table = jax.random.permutation(keys[3], n_pool)[: DEC_BATCH * n_pages]
    page_table = page_table.reshape(DEC_BATCH, n_pages).astype(jnp.int32)
    return {
        'x_prompt': normal(0, (BATCH, SEQ, D_MODEL)),
        'x_sample': normal(1, (DEC_BATCH, DEC_SEQ, D_MODEL)),
        'cache_fox_k': normal(4, (N_EVEN, n_pool, PAGE_SIZE, FOX_HEADS, HEAD_DIM)),
        'cache_fox_v': normal(5, (N_EVEN, n_pool, PAGE_SIZE, FOX_HEADS, HEAD_DIM)),
        'cache_fox_logf': jax.nn.log_sigmoid(normal(6, (N_EVEN, n_pool, PAGE_SIZE, FOX_HEADS)) + FORGET_BIAS_INIT),
        'state_conv': normal(7, (N_EVEN, DEC_BATCH, CONV_WIDTH - 1, CONV_CH), 0.5),
        'cache_sb_k': normal(8, (N_ODD, n_pool, PAGE_SIZE, SB_HEADS, HEAD_DIM)),
        'cache_sb_v': normal(9, (N_ODD, n_pool, PAGE_SIZE, SB_HEADS, HEAD_DIM)),
        'cache_mem_k': normal(10, (DEPTH, DEC_BATCH, N_MEM, MEM_HEADS, MEM_HEAD_DIM)),
        'cache_mem_v': normal(11, (DEPTH, DEC_BATCH, N_MEM, MEM_HEADS, MEM_HEAD_DIM)),
        'state_ffn_conv': normal(12, (DEPTH, DEC_BATCH, FFN_CONV_WIDTH - 1, 2 * D_FF)),
        'page_table': page_table,
        'mem_prompt': normal(2, (BATCH, N_MEM, D_MODEL)),
        'norm_g': 1.0 + normal(13, (DEPTH, N_NORMS, D_MODEL), 0.05),
        'even_w_in': normal(14, (N_EVEN, D_MODEL, EVEN_IN), D_MODEL ** -0.5),
        'even_b_f': FORGET_BIAS_INIT + normal(15, (N_EVEN, FOX_HEADS), 0.5),
        'conv_w': normal(16, (N_EVEN, CONV_WIDTH, CONV_CH), CONV_WIDTH ** -0.5),
        'conv_b': normal(17, (N_EVEN, CONV_CH), 0.02),
        'conv_ln_g': 1.0 + normal(18, (N_EVEN, CONV_CH), 0.05),
        'conv_ln_b': normal(19, (N_EVEN, CONV_CH), 0.02),
        'even_w_out': normal(20, (N_EVEN, CONV_CH + FOX_WIDTH, D_MODEL), (CONV_CH + FOX_WIDTH) ** -0.5),
        'sb_w_in': normal(21, (N_ODD, D_MODEL, 3 * SB_WIDTH), D_MODEL ** -0.5),
        'sb_w_out': normal(22, (N_ODD, SB_WIDTH, D_MODEL), SB_WIDTH ** -0.5),
        'mem_norm_g': 1.0 + normal(23, (DEPTH, D_MODEL), 0.05),
        'mem_wq': normal(24, (DEPTH, D_MODEL, MEM_WIDTH), D_MODEL ** -0.5),
        'mem_wk': normal(25, (DEPTH, D_MODEL, MEM_WIDTH), D_MODEL ** -0.5),
        'mem_wv': normal(26, (DEPTH, D_MODEL, MEM_WIDTH), D_MODEL ** -0.5),
        'mem_wo': normal(27, (DEPTH, MEM_WIDTH, D_MODEL), MEM_WIDTH ** -0.5),
        'ffn_w_up': normal(28, (DEPTH, D_MODEL, 2 * D_FF), D_MODEL ** -0.5),
        'ffn_conv_w': normal(29, (DEPTH, FFN_CONV_WIDTH, 2 * D_FF), FFN_CONV_WIDTH ** -0.5),
        'ffn_conv_b': normal(30, (DEPTH, 2 * D_FF), 0.02),
        'ffn_w_down': normal(31, (DEPTH, D_FF, D_MODEL), D_FF ** -0.5),
    }


def reference(x_prompt, x_sample, cache_fox_k, cache_fox_v, cache_fox_logf, state_conv, cache_sb_k,
              cache_sb_v, cache_mem_k, cache_mem_v, state_ffn_conv, page_table, mem_prompt,
              norm_g, even_w_in, even_b_f, conv_w, conv_b, conv_ln_g, conv_ln_b, even_w_out,
              sb_w_in, sb_w_out, mem_norm_g, mem_wq, mem_wk, mem_wv, mem_wo,
              ffn_w_up, ffn_conv_w, ffn_conv_b, ffn_w_down):
    w = dict(norm_g=norm_g, even_w_in=even_w_in, even_b_f=even_b_f, conv_w=conv_w, conv_b=conv_b,
             conv_ln_g=conv_ln_g, conv_ln_b=conv_ln_b, even_w_out=even_w_out, sb_w_in=sb_w_in,
             sb_w_out=sb_w_out, mem_wq=mem_wq, mem_wo=mem_wo, ffn_w_up=ffn_w_up,
             ffn_conv_w=ffn_conv_w, ffn_conv_b=ffn_conv_b, ffn_w_down=ffn_w_down)

    mem_kv_list = [memory_kv(mem_prompt, mem_norm_g[l], mem_wk[l], mem_wv[l]) for l in range(DEPTH)]
    mem_k_prompt = jnp.stack([kv[0] for kv in mem_kv_list])
    mem_v_prompt = jnp.stack([kv[1] for kv in mem_kv_list])
    bsz = x_prompt.shape[0]
    conv_zero = jnp.zeros((N_EVEN, bsz, CONV_WIDTH - 1, CONV_CH), x_prompt.dtype)
    ffn_zero = jnp.zeros((DEPTH, bsz, FFN_CONV_WIDTH - 1, 2 * D_FF), x_prompt.dtype)
    (y_prompt, fox_k_prompt, fox_v_prompt, fox_logf_prompt, conv_state_prompt,
     sb_k_prompt, sb_v_prompt, ffn_state_prompt) = run_trunk(
        x_prompt, mem_k_prompt, mem_v_prompt, conv_zero, ffn_zero, None, None, None, w)

    (y_sample, fox_k_sample, fox_v_sample, fox_logf_sample, conv_state_sample,
     sb_k_sample, sb_v_sample, ffn_state_sample) = run_trunk(
        x_sample, cache_mem_k, cache_mem_v, state_conv, state_ffn_conv,
        (cache_fox_k, cache_fox_v, cache_fox_logf), (cache_sb_k, cache_sb_v), page_table, w)

    return (y_prompt, y_sample, fox_k_prompt, fox_v_prompt, fox_logf_prompt, conv_state_prompt,
            sb_k_prompt, sb_v_prompt, mem_k_prompt, mem_v_prompt, ffn_state_prompt,
            fox_k_sample, fox_v_sample, fox_logf_sample, conv_state_sample,
            sb_k_sample, sb_v_sample, ffn_state_sample)
```

```python
import functools

import jax
import jax.numpy as jnp
from jax import lax
from jax.experimental import pallas as pl
from jax.experimental.pallas import tpu as pltpu

F32 = jnp.float32
BF16 = jnp.bfloat16

EPS = 1e-6
NEG = -1e30
HEAD_DIM = 64
MEM_HEADS = 4
MEM_HEAD_DIM = 128
CONV_WIDTH = 31
FFN_CONV_WIDTH = 3
LANES = 128
SUBLANES = 8
CONV_HALO = 32
VMEM_LIMIT = 48 * 1024 * 1024


def _cparams(*sem):
    return pltpu.CompilerParams(dimension_semantics=sem, vmem_limit_bytes=VMEM_LIMIT)


def _dot(a, b):
    return jnp.dot(a, b, preferred_element_type=F32)


def _dot_nt(a, b):
    return lax.dot_general(a, b, (((1,), (1,)), ((), ())), preferred_element_type=F32)


def _dot_exact(a, b):
    return jnp.dot(a, b, preferred_element_type=F32, precision=lax.Precision.HIGHEST)


def _rms(x, g):
    return x * lax.rsqrt(jnp.mean(x * x, axis=-1, keepdims=True) + EPS) * g


def _softplus(z):
    return jnp.maximum(z, 0.0) + jnp.log(1.0 + jnp.exp(-jnp.abs(z)))


def _sigmoid(z):
    return 1.0 / (1.0 + jnp.exp(-z))


def _split_dot(x, u):
    hi = x.astype(BF16)
    lo = (x - hi.astype(F32)).astype(BF16)
    return _dot(hi, u) + _dot(lo, u)


def _full(shape):
    nd = len(shape)
    return pl.BlockSpec(shape, lambda *_: (0,) * nd)


def _even_proj_kernel(x_ref, g_ref, w_ref, bf_ref, u_ref, q_ref, k_ref, v_ref, kb_ref, vb_ref,
                      lf_ref, c_ref, carry_ref, *, ch, tiles_per_seq):
    tm = x_ref.shape[0]
    h = _rms(x_ref[...], g_ref[...]).astype(BF16)
    ag = _dot(h, w_ref[:, 0:2 * ch])
    u_ref[...] = ag[:, :ch] * _sigmoid(ag[:, ch:])
    q_ref[...] = (_dot(h, w_ref[:, 2 * ch:3 * ch]) * HEAD_DIM ** -0.5).astype(BF16)
    k = _dot(h, w_ref[:, 3 * ch:4 * ch])
    k_ref[...] = k
    kb_ref[...] = k.astype(BF16)
    v = _dot(h, w_ref[:, 4 * ch:5 * ch])
    v_ref[...] = v
    vb_ref[...] = v.astype(BF16)
    f = _dot(h, w_ref[:, 5 * ch:5 * ch + LANES]) + bf_ref[...]
    lf = -_softplus(-f)
    nh = lf_ref.shape[1]
    lf_ref[...] = lf[:, :nh]
    if tiles_per_seq:
        @pl.when(pl.program_id(0) % tiles_per_seq == 0)
        def _():
            carry_ref[...] = jnp.zeros_like(carry_ref)
        row = lax.broadcasted_iota(jnp.int32, (tm, tm), 0)
        col = lax.broadcasted_iota(jnp.int32, (tm, tm), 1)
        tri = jnp.where(row >= col, 1.0, 0.0).astype(F32)
        c = _dot_exact(tri, lf) + carry_ref[...]
        carry_ref[...] = c[tm - 1:tm, :]
        c_ref[...] = c[:, :nh]
    else:
        c_ref[...] = lf[:, :nh]


def _even_proj(x, g, w_pad, bf_pad, *, ch, nh, seq, tm):
    m, d = x.shape
    tiles_per_seq = seq // tm if seq % tm == 0 else 0
    row = lambda n: pl.BlockSpec((tm, n), lambda i: (i, 0))
    outs = [jax.ShapeDtypeStruct((m, ch), F32), jax.ShapeDtypeStruct((m, ch), BF16),
            jax.ShapeDtypeStruct((m, ch), F32), jax.ShapeDtypeStruct((m, ch), F32),
            jax.ShapeDtypeStruct((m, ch), BF16), jax.ShapeDtypeStruct((m, ch), BF16),
            jax.ShapeDtypeStruct((m, nh), F32), jax.ShapeDtypeStruct((m, nh), F32)]
    return pl.pallas_call(
        functools.partial(_even_proj_kernel, ch=ch, tiles_per_seq=tiles_per_seq),
        grid=(m // tm,),
        in_specs=[row(d), _full((1, d)), _full(w_pad.shape), _full((1, LANES))],
        out_specs=[row(ch)] * 6 + [row(nh)] * 2,
        out_shape=outs,
        scratch_shapes=[pltpu.VMEM((1, LANES), F32)],
        compiler_params=_cparams("arbitrary"),
        name="even_proj",
    )(x, g, w_pad, bf_pad)


def _odd_proj_kernel(x_ref, g_ref, w_ref, q_ref, k_ref, v_ref, kb_ref, vb_ref, *, width):
    h = _rms(x_ref[...], g_ref[...]).astype(BF16)
    q_ref[...] = (_dot(h, w_ref[:, 0:width]) * HEAD_DIM ** -0.5).astype(BF16)
    k = _dot(h, w_ref[:, width:2 * width])
    k_ref[...] = k
    kb_ref[...] = k.astype(BF16)
    v = _dot(h, w_ref[:, 2 * width:3 * width])
    v_ref[...] = v
    vb_ref[...] = v.astype(BF16)


def _odd_proj(x, g, w, *, tm):
    m, d = x.shape
    width = w.shape[1] // 3
    row = lambda n: pl.BlockSpec((tm, n), lambda i: (i, 0))
    outs = [jax.ShapeDtypeStruct((m, width), BF16), jax.ShapeDtypeStruct((m, width), F32),
            jax.ShapeDtypeStruct((m, width), F32), jax.ShapeDtypeStruct((m, width), BF16),
            jax.ShapeDtypeStruct((m, width), BF16)]
    return pl.pallas_call(
        functools.partial(_odd_proj_kernel, width=width),
        grid=(m // tm,),
        in_specs=[row(d), _full((1, d)), _full(w.shape)],
        out_specs=[row(width)] * 5,
        out_shape=outs,
        compiler_params=_cparams("parallel"),
        name="odd_proj",
    )(x, g, w)


def _norm_matmul_kernel(x_ref, g_ref, w_ref, o_ref, *, scale):
    h = _rms(x_ref[...], g_ref[...]).astype(BF16)
    y = _dot(h, w_ref[...])
    if scale != 1.0:
        y = y * scale
    o_ref[...] = y.astype(o_ref.dtype)


def _norm_matmul(x, g, w, *, tm, out_dtype, scale=1.0):
    m, d = x.shape
    n = w.shape[1]
    return pl.pallas_call(
        functools.partial(_norm_matmul_kernel, scale=scale),
        grid=(m // tm,),
        in_specs=[pl.BlockSpec((tm, d), lambda i: (i, 0)), _full((1, d)), _full(w.shape)],
        out_specs=pl.BlockSpec((tm, n), lambda i: (i, 0)),
        out_shape=jax.ShapeDtypeStruct((m, n), out_dtype),
        compiler_params=_cparams("parallel"),
        name="norm_matmul",
    )(x, g, w)


def _linres_kernel(*refs, n_in):
    a_refs, w_refs = refs[:n_in], refs[n_in:2 * n_in]
    x_ref, g_ref, o_ref = refs[2 * n_in:]
    y = _dot(a_refs[0][...], w_refs[0][...])
    for a_ref, w_ref in zip(a_refs[1:], w_refs[1:]):
        y = y + _dot(a_ref[...], w_ref[...])
    o_ref[...] = x_ref[...] + _rms(y, g_ref[...])


def _linear_residual(acts, ws, x, g, *, tm):
    m, d = x.shape
    n_in = len(acts)
    in_specs = [pl.BlockSpec((tm, a.shape[1]), lambda i: (i, 0)) for a in acts]
    in_specs += [_full(w.shape) for w in ws]
    in_specs += [pl.BlockSpec((tm, d), lambda i: (i, 0)), _full((1, d))]
    return pl.pallas_call(
        functools.partial(_linres_kernel, n_in=n_in),
        grid=(m // tm,),
        in_specs=in_specs,
        out_specs=pl.BlockSpec((tm, d), lambda i: (i, 0)),
        out_shape=jax.ShapeDtypeStruct((m, d), F32),
        compiler_params=_cparams("parallel"),
        name="linear_residual",
    )(*acts, *ws, x, g)


def _conv_module_kernel(u_ref, st_ref, w_ref, b_ref, lg_ref, lb_ref, o_ref, full_ref, *, tt, rows):
    t = pl.program_id(1)

    @pl.when(t == 0)
    def _():
        full_ref[0:CONV_HALO, :] = st_ref[0]

    @pl.when(t > 0)
    def _():
        full_ref[0:CONV_HALO, :] = full_ref[tt:tt + CONV_HALO, :]

    full_ref[CONV_HALO:CONV_HALO + tt, :] = u_ref[0]
    first = CONV_HALO - (CONV_WIDTH - 1)
    for r0 in range(0, tt, rows):
        acc = full_ref[r0 + first:r0 + first + rows, :] * w_ref[0:1, :]
        for j in range(1, CONV_WIDTH):
            acc = acc + full_ref[r0 + first + j:r0 + first + j + rows, :] * w_ref[j:j + 1, :]
        y = acc + b_ref[...]
        mu = jnp.mean(y, axis=-1, keepdims=True)
        yc = y - mu
        var = jnp.mean(yc * yc, axis=-1, keepdims=True)
        z = yc * lax.rsqrt(var + EPS) * lg_ref[...] + lb_ref[...]
        o_ref[0, r0:r0 + rows, :] = (z * _sigmoid(z)).astype(o_ref.dtype)


def _conv_module(u, state_pad, w_pad, b, ln_g, ln_b):
    bsz, seq, ch = u.shape
    tt = min(seq, 256)
    rows = min(tt, 32)
    return pl.pallas_call(
        functools.partial(_conv_module_kernel, tt=tt, rows=rows),
        grid=(bsz, seq // tt),
        in_specs=[pl.BlockSpec((1, tt, ch), lambda b_, t: (b_, t, 0)),
                  pl.BlockSpec((1, CONV_HALO, ch), lambda b_, t: (b_, 0, 0)),
                  _full(w_pad.shape), _full((1, ch)), _full((1, ch)), _full((1, ch))],
        out_specs=pl.BlockSpec((1, tt, ch), lambda b_, t: (b_, t, 0)),
        out_shape=jax.ShapeDtypeStruct((bsz, seq, ch), BF16),
        scratch_shapes=[pltpu.VMEM((CONV_HALO + tt, ch), F32)],
        compiler_params=_cparams("parallel", "arbitrary"),
        name="conv_module",
    )(u, state_pad, w_pad, b, ln_g, ln_b)


def _fox_attn_kernel(q_ref, k_ref, v_ref, cc_ref, cr_ref, o_ref, m_ref, l_ref, acc_ref, *, tq):
    qi = pl.program_id(2)
    m_ref[...] = jnp.full_like(m_ref, NEG)
    l_ref[...] = jnp.zeros_like(l_ref)
    acc_ref[...] = jnp.zeros_like(acc_ref)
    cc = cc_ref[0, 0]

    def block(j, masked):
        ks = pl.multiple_of(j * tq, tq)
        for hh in range(2):
            sl = slice(hh * HEAD_DIM, (hh + 1) * HEAD_DIM)
            s = _dot_nt(q_ref[0, :, sl], k_ref[0, pl.ds(ks, tq), sl])
            s = s + (cc[:, hh:hh + 1] - cr_ref[0, 0, hh:hh + 1, pl.ds(ks, tq)])
            if masked:
                row = lax.broadcasted_iota(jnp.int32, (tq, tq), 0)
                col = lax.broadcasted_iota(jnp.int32, (tq, tq), 1)
                s = jnp.where(col <= row, s, NEG)
            m_prev = m_ref[hh]
            m_new = jnp.maximum(m_prev, jnp.max(s, axis=-1, keepdims=True))
            p = jnp.exp(s - m_new)
            alpha = jnp.exp(m_prev - m_new)
            l_ref[hh] = alpha * l_ref[hh] + jnp.sum(p, axis=-1, keepdims=True)
            acc_ref[hh] = alpha * acc_ref[hh] + _dot(p.astype(BF16), v_ref[0, pl.ds(ks, tq), sl])
            m_ref[hh] = m_new

    def body(j, carry):
        block(j, False)
        return carry

    lax.fori_loop(0, qi, body, 0)
    block(qi, True)
    o_ref[0] = jnp.concatenate([acc_ref[hh] / l_ref[hh] for hh in range(2)], axis=-1).astype(o_ref.dtype)


def _fox_attention(q, k, v, c_col, c_row, *, tq):
    bsz, seq, width = q.shape
    pairs = width // (2 * HEAD_DIM)
    return pl.pallas_call(
        functools.partial(_fox_attn_kernel, tq=tq),
        grid=(bsz, pairs, seq // tq),
        in_specs=[pl.BlockSpec((1, tq, LANES), lambda b, h, i: (b, i, h)),
                  pl.BlockSpec((1, seq, LANES), lambda b, h, i: (b, 0, h)),
                  pl.BlockSpec((1, seq, LANES), lambda b, h, i: (b, 0, h)),
                  pl.BlockSpec((1, 1, tq, 2), lambda b, h, i: (b, h, i, 0)),
                  pl.BlockSpec((1, 1, 2, seq), lambda b, h, i: (b, h, 0, 0))],
        out_specs=pl.BlockSpec((1, tq, LANES), lambda b, h, i: (b, i, h)),
        out_shape=jax.ShapeDtypeStruct((bsz, seq, width), BF16),
        scratch_shapes=[pltpu.VMEM((2, tq, 1), F32), pltpu.VMEM((2, tq, 1), F32),
                        pltpu.VMEM((2, tq, HEAD_DIM), F32)],
        compiler_params=_cparams("parallel", "parallel", "arbitrary"),
        name="fox_attention",
    )(q, k, v, c_col, c_row)


def _later_key_matrix(n, dtype):
    row = lax.broadcasted_iota(jnp.int32, (n, n), 0)
    col = lax.broadcasted_iota(jnp.int32, (n, n), 1)
    return jnp.where(row > col, 1.0, 0.0).astype(dtype)


def _sb_block(q, k, v, later, r_prev, valid):
    z = _dot_nt(q, k)
    log_remain = -_softplus(z)
    log_beta = z + log_remain
    if valid is not None:
        log_remain = jnp.where(valid, log_remain, 0.0)
    stick = _split_dot(log_remain, later) + r_prev
    w = jnp.exp(log_beta + stick)
    if valid is not None:
        w = jnp.where(valid, w, 0.0)
    r_new = stick[:, 0:1] + log_remain[:, 0:1]
    return _dot(w.astype(BF16), v), r_new


def _sb_attn_kernel(q_ref, k_ref, v_ref, o_ref, r_ref, acc_ref, *, tq):
    qi = pl.program_id(2)
    later = _later_key_matrix(tq, BF16)
    row = lax.broadcasted_iota(jnp.int32, (tq, tq), 0)
    col = lax.broadcasted_iota(jnp.int32, (tq, tq), 1)
    ks = pl.multiple_of(qi * tq, tq)
    for hh in range(2):
        sl = slice(hh * HEAD_DIM, (hh + 1) * HEAD_DIM)
        o, r = _sb_block(q_ref[0, :, sl], k_ref[0, pl.ds(ks, tq), sl], v_ref[0, pl.ds(ks, tq), sl],
                         later, 0.0, col < row)
        acc_ref[hh] = o
        r_ref[hh] = r

    def body(jj, carry):
        ks_ = pl.multiple_of((qi - 1 - jj) * tq, tq)
        for hh in range(2):
            sl = slice(hh * HEAD_DIM, (hh + 1) * HEAD_DIM)
            o, r = _sb_block(q_ref[0, :, sl], k_ref[0, pl.ds(ks_, tq), sl], v_ref[0, pl.ds(ks_, tq), sl],
                             later, r_ref[hh], None)
            acc_ref[hh] = acc_ref[hh] + o
            r_ref[hh] = r
        return carry

    lax.fori_loop(0, qi, body, 0)
    o_ref[0] = jnp.concatenate([acc_ref[hh] for hh in range(2)], axis=-1).astype(o_ref.dtype)


def _sb_attention(q, k, v, *, tq):
    bsz, seq, width = q.shape
    pairs = width // (2 * HEAD_DIM)
    return pl.pallas_call(
        functools.partial(_sb_attn_kernel, tq=tq),
        grid=(bsz, pairs, seq // tq),
        in_specs=[pl.BlockSpec((1, tq, LANES), lambda b, h, i: (b, i, h)),
                  pl.BlockSpec((1, seq, LANES), lambda b, h, i: (b, 0, h)),
                  pl.BlockSpec((1, seq, LANES), lambda b, h, i: (b, 0, h))],
        out_specs=pl.BlockSpec((1, tq, LANES), lambda b, h, i: (b, i, h)),
        out_shape=jax.ShapeDtypeStruct((bsz, seq, width), BF16),
        scratch_shapes=[pltpu.VMEM((2, tq, 1), F32), pltpu.VMEM((2, tq, HEAD_DIM), F32)],
        compiler_params=_cparams("parallel", "parallel", "arbitrary"),
        name="sb_attention",
    )(q, k, v)


def _mem_attn_kernel(q_ref, k_ref, v_ref, o_ref):
    outs = []
    for h in range(MEM_HEADS):
        sl = slice(h * MEM_HEAD_DIM, (h + 1) * MEM_HEAD_DIM)
        s = _dot_nt(q_ref[0, :, sl].astype(BF16), k_ref[0, :, sl].astype(BF16))
        p = jnp.exp(s - jnp.max(s, axis=-1, keepdims=True))
        o = _dot(p.astype(BF16), v_ref[0, :, sl].astype(BF16))
        outs.append(o / jnp.sum(p, axis=-1, keepdims=True))
    o_ref[0] = jnp.concatenate(outs, axis=-1).astype(o_ref.dtype)


def _mem_attention(q, mk, mv, *, tm):
    bsz, seq, width = q.shape
    n_mem = mk.shape[1]
    return pl.pallas_call(
        _mem_attn_kernel,
        grid=(bsz, seq // tm),
        in_specs=[pl.BlockSpec((1, tm, width), lambda b, i: (b, i, 0)),
                  pl.BlockSpec((1, n_mem, width), lambda b, i: (b, 0, 0)),
                  pl.BlockSpec((1, n_mem, width), lambda b, i: (b, 0, 0))],
        out_specs=pl.BlockSpec((1, tm, width), lambda b, i: (b, i, 0)),
        out_shape=jax.ShapeDtypeStruct((bsz, seq, width), BF16),
        compiler_params=_cparams("parallel", "parallel"),
        name="mem_attention",
    )(q, mk, mv)


def _ffn_kernel(*refs, seq, tiles_per_seq):
    (x_ref, g4_ref, g5_ref, wg_ref, wv_ref, cwg_ref, cwv_ref, cbg_ref, cbv_ref, wd_ref) = refs[:10]
    if tiles_per_seq:
        sg_ref, sv_ref, o_ref, og_ref, ov_ref, h_ref, acc_ref, cg_ref, cv_ref = refs[10:]
    else:
        s1g_ref, s1v_ref, s2g_ref, s2v_ref, o_ref, og_ref, ov_ref, h_ref, acc_ref = refs[10:]
    i, j = pl.program_id(0), pl.program_id(1)
    tm = x_ref.shape[0]
    tf = wg_ref.shape[1]

    @pl.when(j == 0)
    def _():
        h_ref[...] = _rms(x_ref[...], g4_ref[...]).astype(BF16)
        acc_ref[...] = jnp.zeros_like(acc_ref)

    h = h_ref[...]
    rmod = lax.broadcasted_iota(jnp.int32, (tm, tf), 0) % seq

    def conv(u, w_ref, b_ref, prev1, prev2):
        s1 = jnp.where(rmod >= 1, pltpu.roll(u, 1, 0), prev1)
        s2 = jnp.where(rmod >= 2, pltpu.roll(u, 2, 0), prev2)
        return s2 * w_ref[0:1, :] + s1 * w_ref[1:2, :] + u * w_ref[2:3, :] + b_ref[...]

    ug = _dot(h, wg_ref[...])
    uv = _dot(h, wv_ref[...])
    if tiles_per_seq:
        @pl.when(i % tiles_per_seq == 0)
        def _():
            cg_ref[j] = sg_ref[0]
            cv_ref[j] = sv_ref[0]

        pg = cg_ref[j]
        pv = cv_ref[j]
        cg_ref[j] = ug[tm - 2:tm, :]
        cv_ref[j] = uv[tm - 2:tm, :]
        og_ref[0] = ug[tm - 2:tm, :]
        ov_ref[0] = uv[tm - 2:tm, :]
        cgate = conv(ug, cwg_ref, cbg_ref, pg[1:2, :], jnp.where(rmod == 0, pg[0:1, :], pg[1:2, :]))
        cval = conv(uv, cwv_ref, cbv_ref, pv[1:2, :], jnp.where(rmod == 0, pv[0:1, :], pv[1:2, :]))
    else:
        og_ref[...] = ug
        ov_ref[...] = uv
        cgate = conv(ug, cwg_ref, cbg_ref, s1g_ref[...], s2g_ref[...])
        cval = conv(uv, cwv_ref, cbv_ref, s1v_ref[...], s2v_ref[...])
    act = (cgate * _sigmoid(cgate) * cval).astype(BF16)
    acc_ref[...] += _dot(act, wd_ref[...])

    @pl.when(j == pl.num_programs(1) - 1)
    def _():
        o_ref[...] = x_ref[...] + _rms(acc_ref[...], g5_ref[...])


def _conv_ffn(x, g4, g5, w_up, conv_w, conv_b, w_down, state, *, bsz, seq, tm, tf):
    m, d = x.shape
    d_ff = w_down.shape[0]
    nf = d_ff // tf
    tiles_per_seq = seq // tm if seq % tm == 0 else 0
    kw = FFN_CONV_WIDTH
    in_specs = [pl.BlockSpec((tm, d), lambda i, j: (i, 0)), _full((1, d)), _full((1, d)),
                pl.BlockSpec((d, tf), lambda i, j: (0, j)),
                pl.BlockSpec((d, tf), lambda i, j: (0, nf + j)),
                pl.BlockSpec((kw, tf), lambda i, j: (0, j)),
                pl.BlockSpec((kw, tf), lambda i, j: (0, nf + j)),
                pl.BlockSpec((1, tf), lambda i, j: (0, j)),
                pl.BlockSpec((1, tf), lambda i, j: (0, nf + j)),
                pl.BlockSpec((tf, d), lambda i, j: (j, 0))]
    args = [x, g4, g5, w_up, w_up, conv_w, conv_w, conv_b, conv_b, w_down]
    scratch = [pltpu.VMEM((tm, d), BF16), pltpu.VMEM((tm, d), F32)]
    x_spec = pl.BlockSpec((tm, d), lambda i, j: (i, 0))
    if tiles_per_seq:
        in_specs += [pl.BlockSpec((1, kw - 1, tf), lambda i, j: (i // tiles_per_seq, 0, j)),
                     pl.BlockSpec((1, kw - 1, tf), lambda i, j: (i // tiles_per_seq, 0, nf + j))]
        args += [state, state]
        st_spec = pl.BlockSpec((1, kw - 1, tf), lambda i, j: (i // tiles_per_seq, 0, j))
        out_specs = [x_spec, st_spec, st_spec]
        out_shape = [jax.ShapeDtypeStruct((m, d), F32),
                     jax.ShapeDtypeStruct((bsz, kw - 1, d_ff), F32),
                     jax.ShapeDtypeStruct((bsz, kw - 1, d_ff), F32)]
        scratch += [pltpu.VMEM((nf, kw - 1, tf), F32), pltpu.VMEM((nf, kw - 1, tf), F32)]
    else:
        assert tm % seq == 0 and seq >= kw - 1
        pad = seq - (kw - 1)
        s2 = jnp.pad(state, ((0, 0), (0, pad), (0, 0))).reshape(m, 2 * d_ff)
        s1 = jnp.pad(state[:, 1:], ((0, 0), (0, seq - 1), (0, 0))).reshape(m, 2 * d_ff)
        in_specs += [pl.BlockSpec((tm, tf), lambda i, j: (i, j)),
                     pl.BlockSpec((tm, tf), lambda i, j: (i, nf + j)),
                     pl.BlockSpec((tm, tf), lambda i, j: (i, j)),
                     pl.BlockSpec((tm, tf), lambda i, j: (i, nf + j))]
        args += [s1, s1, s2, s2]
        u_spec = pl.BlockSpec((tm, tf), lambda i, j: (i, j))
        out_specs = [x_spec, u_spec, u_spec]
        out_shape = [jax.ShapeDtypeStruct((m, d), F32), jax.ShapeDtypeStruct((m, d_ff), F32),
                     jax.ShapeDtypeStruct((m, d_ff), F32)]
    y, og, ov = pl.pallas_call(
        functools.partial(_ffn_kernel, seq=seq, tiles_per_seq=tiles_per_seq),
        grid=(m // tm, nf),
        in_specs=in_specs,
        out_specs=out_specs,
        out_shape=out_shape,
        scratch_shapes=scratch,
        compiler_params=_cparams("arbitrary", "arbitrary"),
        name="conv_ffn",
    )(*args)
    if not tiles_per_seq:
        og = og.reshape(bsz, seq, d_ff)[:, seq - (kw - 1):]
        ov = ov.reshape(bsz, seq, d_ff)[:, seq - (kw - 1):]
    return y, jnp.concatenate([og, ov], axis=-1)


def _block_diag_queries(q, n_heads):
    t, width = q.shape
    tiled = jnp.concatenate([q.astype(F32)] * n_heads, axis=0)
    row = lax.broadcasted_iota(jnp.int32, (n_heads * t, width), 0)
    col = lax.broadcasted_iota(jnp.int32, (n_heads * t, width), 1)
    return jnp.where(row // t == col // HEAD_DIM, tiled, 0.0).astype(BF16)


def _diag_blocks(acc, t, n_heads):
    return jnp.concatenate(
        [acc[h * t:(h + 1) * t, h * HEAD_DIM:(h + 1) * HEAD_DIM] for h in range(n_heads)], axis=-1)


def _fox_decode_kernel(pt_ref, q_ref, kn_ref, vn_ref, lq_ref, *refs, pages_per_step, n_heads, t_new):
    npg = pages_per_step
    k_refs, v_refs, lf_refs = refs[:npg], refs[npg:2 * npg], refs[2 * npg:3 * npg]
    o_ref, qbd_ref, m_ref, l_ref, acc_ref, carry_ref, gq_ref = refs[3 * npg:]
    step = pl.program_id(1)
    rows = n_heads * t_new
    later = _later_key_matrix(LANES, F32)
    lane = lax.broadcasted_iota(jnp.int32, (rows, LANES), 1)
    qpos = lax.broadcasted_iota(jnp.int32, (rows, LANES), 0) % t_new

    def update(s, v):
        m_prev = m_ref[...]
        m_new = jnp.maximum(m_prev, jnp.max(s, axis=-1, keepdims=True))
        p = jnp.exp(s - m_new)
        alpha = jnp.exp(m_prev - m_new)
        l_ref[...] = alpha * l_ref[...] + jnp.sum(p, axis=-1, keepdims=True)
        acc_ref[...] = alpha * acc_ref[...] + _dot(p.astype(BF16), v)
        m_ref[...] = m_new

    @pl.when(step == 0)
    def _():
        qbd_ref[...] = _block_diag_queries(q_ref[0], n_heads)
        m_ref[...] = jnp.full_like(m_ref, NEG)
        l_ref[...] = jnp.zeros_like(l_ref)
        acc_ref[...] = jnp.zeros_like(acc_ref)
        logf_rows = lq_ref[0]
        gap = _dot_exact(logf_rows, later)
        carry_ref[...] = gap[:, 0:1] + logf_rows[:, 0:1]
        gq_ref[...] = jnp.sum(jnp.where(lane == qpos, gap, 0.0), axis=-1, keepdims=True)
        s = (gap - gq_ref[...]) + _dot_nt(qbd_ref[...], kn_ref[0])
        update(jnp.where(lane <= qpos, s, NEG), vn_ref[0])

    for p_ in range(npg):
        lf_t = lf_refs[p_][0]
        logf_rows = jnp.concatenate(
            [jnp.broadcast_to(lf_t[h:h + 1, :], (t_new, LANES)) for h in range(n_heads)], axis=0)
        gap = _dot_exact(logf_rows, later) + carry_ref[...]
        carry_ref[...] = gap[:, 0:1] + logf_rows[:, 0:1]
        s = (gap - gq_ref[...]) + _dot_nt(qbd_ref[...], k_refs[p_][0].astype(BF16))
        update(s, v_refs[p_][0].astype(BF16))

    @pl.when(step == pl.num_programs(1) - 1)
    def _():
        o_ref[0] = _diag_blocks(acc_ref[...] / l_ref[...], t_new, n_heads).astype(o_ref.dtype)


def _sb_decode_kernel(pt_ref, q_ref, kn_ref, vn_ref, *refs, pages_per_step, n_heads, t_new):
    npg = pages_per_step
    k_refs, v_refs = refs[:npg], refs[npg:2 * npg]
    o_ref, qbd_ref, r_ref, acc_ref = refs[2 * npg:]
    step = pl.program_id(1)
    rows = n_heads * t_new
    later = _later_key_matrix(LANES, BF16)

    @pl.when(step == 0)
    def _():
        qbd_ref[...] = _block_diag_queries(q_ref[0], n_heads)
        lane = lax.broadcasted_iota(jnp.int32, (rows, LANES), 1)
        qpos = lax.broadcasted_iota(jnp.int32, (rows, LANES), 0) % t_new
        o, r = _sb_block(qbd_ref[...], kn_ref[0], vn_ref[0], later, 0.0, lane < qpos)
        acc_ref[...] = o
        r_ref[...] = r

    for p_ in range(npg):
        o, r = _sb_block(qbd_ref[...], k_refs[p_][0].astype(BF16), v_refs[p_][0].astype(BF16),
                         later, r_ref[...], None)
        acc_ref[...] += o
        r_ref[...] = r

    @pl.when(step == pl.num_programs(1) - 1)
    def _():
        o_ref[0] = _diag_blocks(acc_ref[...], t_new, n_heads).astype(o_ref.dtype)


def _decode_attention(kind, page_table, q, k_new, v_new, k_pool, v_pool, logf_new=None, logf_pool=None,
                      *, pages_per_step):
    bsz, t_new, width = q.shape
    n_heads = width // HEAD_DIM
    n_pages = page_table.shape[1]
    page = k_pool.shape[1]
    assert page == LANES and t_new <= page and n_pages % pages_per_step == 0
    n_steps = n_pages // pages_per_step
    rows = n_heads * t_new
    pad = ((0, 0), (0, page - t_new), (0, 0))
    kn, vn = jnp.pad(k_new, pad), jnp.pad(v_new, pad)

    def page_spec(shape, p_):
        def index(b, s, pt):
            return (pt[b, n_pages - 1 - (s * pages_per_step + p_)], 0, 0)
        return pl.BlockSpec(shape, index)

    per_seq = lambda shape: pl.BlockSpec(shape, lambda b, s, pt: (b, 0, 0))
    in_specs = [per_seq((1, t_new, width)), per_seq((1, page, width)), per_seq((1, page, width))]
    args = [q, kn, vn]
    kv_specs = [page_spec((1, page, width), p_) for p_ in range(pages_per_step)]
    if kind == "fox":
        lq = jnp.repeat(jnp.swapaxes(logf_new, 1, 2), t_new, axis=1)
        lq = jnp.pad(lq, ((0, 0), (0, 0), (0, page - t_new)))
        lf_t = jnp.swapaxes(logf_pool, 1, 2)
        in_specs += [per_seq((1, rows, page))] + kv_specs + kv_specs
        in_specs += [page_spec((1, n_heads, page), p_) for p_ in range(pages_per_step)]
        args += [lq] + [k_pool] * pages_per_step + [v_pool] * pages_per_step + [lf_t] * pages_per_step
        body = _fox_decode_kernel
        scratch = [pltpu.VMEM((rows, width), BF16), pltpu.VMEM((rows, 1), F32), pltpu.VMEM((rows, 1), F32),
                   pltpu.VMEM((rows, width), F32), pltpu.VMEM((rows, 1), F32), pltpu.VMEM((rows, 1), F32)]
    else:
        in_specs += kv_specs + kv_specs
        args += [k_pool] * pages_per_step + [v_pool] * pages_per_step
        body = _sb_decode_kernel
        scratch = [pltpu.VMEM((rows, width), BF16), pltpu.VMEM((rows, 1), F32),
                   pltpu.VMEM((rows, width), F32)]
    return pl.pallas_call(
        functools.partial(body, pages_per_step=pages_per_step, n_heads=n_heads, t_new=t_new),
        grid_spec=pltpu.PrefetchScalarGridSpec(
            num_scalar_prefetch=1,
            grid=(bsz, n_steps),
            in_specs=in_specs,
            out_specs=pl.BlockSpec((1, t_new, width), lambda b, s, pt: (b, 0, 0)),
            scratch_shapes=scratch),
        out_shape=jax.ShapeDtypeStruct((bsz, t_new, width), F32),
        compiler_params=_cparams("parallel", "arbitrary"),
        name=kind + "_decode_attention",
    )(page_table, *args)


def _prep_weights(p):
    w = {}
    n_even = p["even_w_in"].shape[0]
    ch = p["conv_w"].shape[2]
    even_in = p["even_w_in"].shape[2]
    n_fox = even_in - 5 * ch
    w["even_w_in"] = jnp.pad(p["even_w_in"], ((0, 0), (0, 0), (0, LANES - n_fox))).astype(BF16)
    w["even_b_f"] = jnp.pad(p["even_b_f"], ((0, 0), (0, LANES - n_fox))).reshape(n_even, 1, LANES)
    w["conv_w"] = jnp.pad(p["conv_w"], ((0, 0), (0, CONV_HALO - CONV_WIDTH), (0, 0)))
    for name in ("even_w_out", "sb_w_in", "sb_w_out", "mem_wq", "mem_wo", "ffn_w_up", "ffn_w_down"):
        w[name] = p[name].astype(BF16)
    w["mem_wkv"] = jnp.concatenate([p["mem_wk"], p["mem_wv"]], axis=-1).astype(BF16)
    return w


def _run_trunk(x, mem_k, mem_v, conv_state, ffn_state, fox_cache, sb_cache, page_table, p, w):
    bsz, seq, d = x.shape
    m = bsz * seq
    depth = p["norm_g"].shape[0]
    ch = p["conv_w"].shape[2]
    n_fox = p["even_b_f"].shape[1]
    prompt = fox_cache is None
    tm = min(m, 256)
    tq = min(seq, 256)
    x = x.reshape(m, d)
    fox_k, fox_v, fox_logf, conv_new, sb_k, sb_v, ffn_new = [], [], [], [], [], [], []
    for layer in range(depth):
        g = p["norm_g"][layer].reshape(6, 1, d)
        i = layer // 2
        if layer % 2 == 0:
            u, q, k, v, kb, vb, logf, cum = _even_proj(
                x, g[0], w["even_w_in"][i], w["even_b_f"][i], ch=ch, nh=n_fox, seq=seq, tm=tm)
            u3 = u.reshape(bsz, seq, ch)
            state = conv_state[i]
            halo = CONV_WIDTH - 1
            state_pad = jnp.pad(state, ((0, 0), (CONV_HALO - halo, 0), (0, 0)))
            a_out = _conv_module(u3, state_pad, w["conv_w"][i], p["conv_b"][i].reshape(1, ch),
                                 p["conv_ln_g"][i].reshape(1, ch), p["conv_ln_b"][i].reshape(1, ch))
            conv_new.append(jnp.concatenate([state, u3], axis=1)[:, seq:])
            q3, kb3, vb3 = (a.reshape(bsz, seq, ch) for a in (q, kb, vb))
            if prompt:
                c4 = cum.reshape(bsz, seq, n_fox // 2, 2)
                o = _fox_attention(q3, kb3, vb3, jnp.transpose(c4, (0, 2, 1, 3)),
                                   jnp.transpose(c4, (0, 2, 3, 1)), tq=tq)
            else:
                pool = lambda c: c[i].reshape(c.shape[1], c.shape[2], -1)
                o = _decode_attention("fox", page_table, q3.astype(F32), kb3, vb3, pool(fox_cache[0]), pool(fox_cache[1]),
                                      logf.reshape(bsz, seq, n_fox), fox_cache[2][i], pages_per_step=4)
            x = _linear_residual([a_out.reshape(m, ch), o.reshape(m, ch).astype(BF16)],
                                 [w["even_w_out"][i][:ch], w["even_w_out"][i][ch:]], x, g[1], tm=tm)
            fox_k.append(k.reshape(bsz, seq, n_fox, HEAD_DIM))
            fox_v.append(v.reshape(bsz, seq, n_fox, HEAD_DIM))
            fox_logf.append(logf.reshape(bsz, seq, n_fox))
        else:
            q, k, v, kb, vb = _odd_proj(x, g[0], w["sb_w_in"][i], tm=tm)
            width = q.shape[1]
            q3, kb3, vb3 = (a.reshape(bsz, seq, width) for a in (q, kb, vb))
            if prompt:
                o = _sb_attention(q3, kb3, vb3, tq=tq)
            else:
                pool = lambda c: c[i].reshape(c.shape[1], c.shape[2], -1)
                o = _decode_attention("sb", page_table, q3.astype(F32), kb3, vb3, pool(sb_cache[0]), pool(sb_cache[1]),
                                      pages_per_step=4)
            x = _linear_residual([o.reshape(m, width).astype(BF16)], [w["sb_w_out"][i]], x, g[1], tm=tm)
            sb_k.append(k.reshape(bsz, seq, width // HEAD_DIM, HEAD_DIM))
            sb_v.append(v.reshape(bsz, seq, width // HEAD_DIM, HEAD_DIM))
        qm = _norm_matmul(x, g[2], w["mem_wq"][layer], tm=tm, out_dtype=BF16 if seq % 16 == 0 else F32,
                          scale=MEM_HEAD_DIM ** -0.5)
        mw = qm.shape[1]
        om = _mem_attention(qm.reshape(bsz, seq, mw), mem_k[layer], mem_v[layer], tm=min(seq, 512))
        x = _linear_residual([om.reshape(m, mw)], [w["mem_wo"][layer]], x, g[3], tm=tm)
        d_ff2 = p["ffn_w_up"].shape[2]
        x, buf = _conv_ffn(x, g[4], g[5], w["ffn_w_up"][layer], p["ffn_conv_w"][layer],
                           p["ffn_conv_b"][layer].reshape(1, d_ff2), w["ffn_w_down"][layer],
                           ffn_state[layer], bsz=bsz, seq=seq, tm=min(m, 512), tf=256)
        ffn_new.append(buf)
    return (x.reshape(bsz, seq, d), jnp.stack(fox_k), jnp.stack(fox_v), jnp.stack(fox_logf),
            jnp.stack(conv_new), jnp.stack(sb_k), jnp.stack(sb_v), jnp.stack(ffn_new))


def kernel(x_prompt, x_sample, cache_fox_k, cache_fox_v, cache_fox_logf, state_conv, cache_sb_k, cache_sb_v, cache_mem_k, cache_mem_v, state_ffn_conv, page_table, mem_prompt, norm_g, even_w_in, even_b_f, conv_w, conv_b, conv_ln_g, conv_ln_b, even_w_out, sb_w_in, sb_w_out, mem_norm_g, mem_wq, mem_wk, mem_wv, mem_wo, ffn_w_up, ffn_conv_w, ffn_conv_b, ffn_w_down):
    p = dict(norm_g=norm_g, even_w_in=even_w_in, even_b_f=even_b_f, conv_w=conv_w, conv_b=conv_b,
             conv_ln_g=conv_ln_g, conv_ln_b=conv_ln_b, even_w_out=even_w_out, sb_w_in=sb_w_in,
             sb_w_out=sb_w_out, mem_wq=mem_wq, mem_wk=mem_wk, mem_wv=mem_wv, mem_wo=mem_wo,
             ffn_w_up=ffn_w_up, ffn_conv_w=ffn_conv_w, ffn_conv_b=ffn_conv_b, ffn_w_down=ffn_w_down)
    w = _prep_weights(p)
    depth, d = norm_g.shape[0], norm_g.shape[2]
    bsz, n_mem = mem_prompt.shape[0], mem_prompt.shape[1]
    mem_width = mem_wq.shape[2]
    n_even = even_w_in.shape[0]
    ch = conv_w.shape[2]

    mem_flat = mem_prompt.reshape(bsz * n_mem, d)
    mem_kv = [_norm_matmul(mem_flat, mem_norm_g[l].reshape(1, d), w["mem_wkv"][l], tm=256, out_dtype=F32)
              for l in range(depth)]
    mem_k_prompt = jnp.stack([kv[:, :mem_width].reshape(bsz, n_mem, mem_width) for kv in mem_kv])
    mem_v_prompt = jnp.stack([kv[:, mem_width:].reshape(bsz, n_mem, mem_width) for kv in mem_kv])
    conv_zero = jnp.zeros((n_even, bsz, CONV_WIDTH - 1, ch), F32)
    ffn_zero = jnp.zeros((depth, bsz, FFN_CONV_WIDTH - 1, ffn_w_up.shape[2]), F32)
    (y_prompt, fox_k_prompt, fox_v_prompt, fox_logf_prompt, conv_state_prompt,
     sb_k_prompt, sb_v_prompt, ffn_state_prompt) = _run_trunk(
        x_prompt, mem_k_prompt, mem_v_prompt, conv_zero, ffn_zero, None, None, None, p, w)

    dbsz = x_sample.shape[0]
    cmk = cache_mem_k.reshape(depth, dbsz, n_mem, mem_width)
    cmv = cache_mem_v.reshape(depth, dbsz, n_mem, mem_width)
    (y_sample, fox_k_sample, fox_v_sample, fox_logf_sample, conv_state_sample,
     sb_k_sample, sb_v_sample, ffn_state_sample) = _run_trunk(
        x_sample, cmk, cmv, state_conv, state_ffn_conv,
        (cache_fox_k, cache_fox_v, cache_fox_logf), (cache_sb_k, cache_sb_v), page_table, p, w)

    mem_shape = (depth, bsz, n_mem, MEM_HEADS, MEM_HEAD_DIM)
    return (y_prompt, y_sample, fox_k_prompt, fox_v_prompt, fox_logf_prompt, conv_state_prompt,
            sb_k_prompt, sb_v_prompt, mem_k_prompt.reshape(mem_shape), mem_v_prompt.reshape(mem_shape),
            ffn_state_prompt, fox_k_sample, fox_v_sample, fox_logf_sample, conv_state_sample,
            sb_k_sample, sb_v_sample, ffn_state_sample)
```

```python
import functools

import jax
import jax.numpy as jnp
from jax import lax
from jax.experimental import pallas as pl
from jax.experimental.pallas import tpu as pltpu

F32 = jnp.float32
BF16 = jnp.bfloat16

EPS = 1e-6
NEG = -1e30
SB_DEAD = -105.0
HEAD_DIM = 64
MEM_HEADS = 4
MEM_HEAD_DIM = 128
CONV_WIDTH = 31
FFN_CONV_WIDTH = 3
LANES = 128
SUBLANES = 8
CONV_HALO = 32
VMEM_LIMIT = 48 * 1024 * 1024


def _cparams(*sem):
    return pltpu.CompilerParams(dimension_semantics=sem, vmem_limit_bytes=VMEM_LIMIT)


def _dot(a, b):
    return jnp.dot(a, b, preferred_element_type=F32)


def _dot_nt(a, b):
    return lax.dot_general(a, b, (((1,), (1,)), ((), ())), preferred_element_type=F32)


def _dot_exact(a, b):
    return jnp.dot(a, b, preferred_element_type=F32, precision=lax.Precision.HIGHEST)


def _rms(x, g):
    return x * lax.rsqrt(jnp.mean(x * x, axis=-1, keepdims=True) + EPS) * g


def _softplus(z):
    return jnp.maximum(z, 0.0) + jnp.log(1.0 + jnp.exp(-jnp.abs(z)))


def _sigmoid(z):
    return 1.0 / (1.0 + jnp.exp(-z))


def _split_dot(x, u):
    hi = x.astype(BF16)
    lo = (x - hi.astype(F32)).astype(BF16)
    return _dot(hi, u) + _dot(lo, u)


def _full(shape):
    nd = len(shape)
    return pl.BlockSpec(shape, lambda *_: (0,) * nd)


def _even_proj_kernel(x_ref, g_ref, w_ref, bf_ref, u_ref, q_ref, k_ref, v_ref, kb_ref, vb_ref,
                      lf_ref, c_ref, carry_ref, *, ch, tiles_per_seq):
    tm = x_ref.shape[0]
    h = _rms(x_ref[...], g_ref[...]).astype(BF16)
    ag = _dot(h, w_ref[:, 0:2 * ch])
    u_ref[...] = ag[:, :ch] * _sigmoid(ag[:, ch:])
    q_ref[...] = (_dot(h, w_ref[:, 2 * ch:3 * ch]) * HEAD_DIM ** -0.5).astype(BF16)
    k = _dot(h, w_ref[:, 3 * ch:4 * ch])
    k_ref[...] = k
    kb_ref[...] = k.astype(BF16)
    v = _dot(h, w_ref[:, 4 * ch:5 * ch])
    v_ref[...] = v
    vb_ref[...] = v.astype(BF16)
    f = _dot(h, w_ref[:, 5 * ch:5 * ch + LANES]) + bf_ref[...]
    lf = -_softplus(-f)
    nh = lf_ref.shape[1]
    lf_ref[...] = lf[:, :nh]
    if tiles_per_seq:
        @pl.when(pl.program_id(0) % tiles_per_seq == 0)
        def _():
            carry_ref[...] = jnp.zeros_like(carry_ref)
        row = lax.broadcasted_iota(jnp.int32, (tm, tm), 0)
        col = lax.broadcasted_iota(jnp.int32, (tm, tm), 1)
        tri = jnp.where(row >= col, 1.0, 0.0).astype(F32)
        c = _dot_exact(tri, lf) + carry_ref[...]
        carry_ref[...] = c[tm - 1:tm, :]
        c_ref[...] = c[:, :nh]
    else:
        c_ref[...] = lf[:, :nh]


def _even_proj(x, g, w_pad, bf_pad, *, ch, nh, seq, tm):
    m, d = x.shape
    tiles_per_seq = seq // tm if seq % tm == 0 else 0
    row = lambda n: pl.BlockSpec((tm, n), lambda i: (i, 0))
    outs = [jax.ShapeDtypeStruct((m, ch), F32), jax.ShapeDtypeStruct((m, ch), BF16),
            jax.ShapeDtypeStruct((m, ch), F32), jax.ShapeDtypeStruct((m, ch), F32),
            jax.ShapeDtypeStruct((m, ch), BF16), jax.ShapeDtypeStruct((m, ch), BF16),
            jax.ShapeDtypeStruct((m, nh), F32), jax.ShapeDtypeStruct((m, nh), F32)]
    return pl.pallas_call(
        functools.partial(_even_proj_kernel, ch=ch, tiles_per_seq=tiles_per_seq),
        grid=(m // tm,),
        in_specs=[row(d), _full((1, d)), _full(w_pad.shape), _full((1, LANES))],
        out_specs=[row(ch)] * 6 + [row(nh)] * 2,
        out_shape=outs,
        scratch_shapes=[pltpu.VMEM((1, LANES), F32)],
        compiler_params=_cparams("arbitrary"),
        name="even_proj",
    )(x, g, w_pad, bf_pad)


def _odd_proj_kernel(x_ref, g_ref, w_ref, q_ref, k_ref, v_ref, kb_ref, vb_ref, *, width):
    h = _rms(x_ref[...], g_ref[...]).astype(BF16)
    q_ref[...] = (_dot(h, w_ref[:, 0:width]) * HEAD_DIM ** -0.5).astype(BF16)
    k = _dot(h, w_ref[:, width:2 * width])
    k_ref[...] = k
    kb_ref[...] = k.astype(BF16)
    v = _dot(h, w_ref[:, 2 * width:3 * width])
    v_ref[...] = v
    vb_ref[...] = v.astype(BF16)


def _odd_proj(x, g, w, *, tm):
    m, d = x.shape
    width = w.shape[1] // 3
    row = lambda n: pl.BlockSpec((tm, n), lambda i: (i, 0))
    outs = [jax.ShapeDtypeStruct((m, width), BF16), jax.ShapeDtypeStruct((m, width), F32),
            jax.ShapeDtypeStruct((m, width), F32), jax.ShapeDtypeStruct((m, width), BF16),
            jax.ShapeDtypeStruct((m, width), BF16)]
    return pl.pallas_call(
        functools.partial(_odd_proj_kernel, width=width),
        grid=(m // tm,),
        in_specs=[row(d), _full((1, d)), _full(w.shape)],
        out_specs=[row(width)] * 5,
        out_shape=outs,
        compiler_params=_cparams("parallel"),
        name="odd_proj",
    )(x, g, w)


def _norm_matmul_kernel(x_ref, g_ref, w_ref, o_ref, *, scale):
    h = _rms(x_ref[...], g_ref[...]).astype(BF16)
    y = _dot(h, w_ref[...])
    if scale != 1.0:
        y = y * scale
    o_ref[...] = y.astype(o_ref.dtype)


def _norm_matmul(x, g, w, *, tm, out_dtype, scale=1.0):
    m, d = x.shape
    n = w.shape[1]
    return pl.pallas_call(
        functools.partial(_norm_matmul_kernel, scale=scale),
        grid=(m // tm,),
        in_specs=[pl.BlockSpec((tm, d), lambda i: (i, 0)), _full((1, d)), _full(w.shape)],
        out_specs=pl.BlockSpec((tm, n), lambda i: (i, 0)),
        out_shape=jax.ShapeDtypeStruct((m, n), out_dtype),
        compiler_params=_cparams("parallel"),
        name="norm_matmul",
    )(x, g, w)


def _linres_kernel(*refs, n_in):
    a_refs, w_refs = refs[:n_in], refs[n_in:2 * n_in]
    x_ref, g_ref, o_ref = refs[2 * n_in:]
    y = _dot(a_refs[0][...], w_refs[0][...])
    for a_ref, w_ref in zip(a_refs[1:], w_refs[1:]):
        y = y + _dot(a_ref[...], w_ref[...])
    o_ref[...] = x_ref[...] + _rms(y, g_ref[...])


def _linear_residual(acts, ws, x, g, *, tm):
    m, d = x.shape
    n_in = len(acts)
    in_specs = [pl.BlockSpec((tm, a.shape[1]), lambda i: (i, 0)) for a in acts]
    in_specs += [_full(w.shape) for w in ws]
    in_specs += [pl.BlockSpec((tm, d), lambda i: (i, 0)), _full((1, d))]
    return pl.pallas_call(
        functools.partial(_linres_kernel, n_in=n_in),
        grid=(m // tm,),
        in_specs=in_specs,
        out_specs=pl.BlockSpec((tm, d), lambda i: (i, 0)),
        out_shape=jax.ShapeDtypeStruct((m, d), F32),
        compiler_params=_cparams("parallel"),
        name="linear_residual",
    )(*acts, *ws, x, g)


def _conv_module_kernel(u_ref, st_ref, w_ref, b_ref, lg_ref, lb_ref, o_ref, full_ref, *, tt, rows):
    t = pl.program_id(1)

    @pl.when(t == 0)
    def _():
        full_ref[0:CONV_HALO, :] = st_ref[0]

    @pl.when(t > 0)
    def _():
        full_ref[0:CONV_HALO, :] = full_ref[tt:tt + CONV_HALO, :]

    full_ref[CONV_HALO:CONV_HALO + tt, :] = u_ref[0]
    first = CONV_HALO - (CONV_WIDTH - 1)
    for r0 in range(0, tt, rows):
        acc = full_ref[r0 + first:r0 + first + rows, :] * w_ref[0:1, :]
        for j in range(1, CONV_WIDTH):
            acc = acc + full_ref[r0 + first + j:r0 + first + j + rows, :] * w_ref[j:j + 1, :]
        y = acc + b_ref[...]
        mu = jnp.mean(y, axis=-1, keepdims=True)
        yc = y - mu
        var = jnp.mean(yc * yc, axis=-1, keepdims=True)
        z = yc * lax.rsqrt(var + EPS) * lg_ref[...] + lb_ref[...]
        o_ref[0, r0:r0 + rows, :] = (z * _sigmoid(z)).astype(o_ref.dtype)


def _conv_module(u, state_pad, w_pad, b, ln_g, ln_b):
    bsz, seq, ch = u.shape
    tt = min(seq, 256)
    rows = min(tt, 32)
    return pl.pallas_call(
        functools.partial(_conv_module_kernel, tt=tt, rows=rows),
        grid=(bsz, seq // tt),
        in_specs=[pl.BlockSpec((1, tt, ch), lambda b_, t: (b_, t, 0)),
                  pl.BlockSpec((1, CONV_HALO, ch), lambda b_, t: (b_, 0, 0)),
                  _full(w_pad.shape), _full((1, ch)), _full((1, ch)), _full((1, ch))],
        out_specs=pl.BlockSpec((1, tt, ch), lambda b_, t: (b_, t, 0)),
        out_shape=jax.ShapeDtypeStruct((bsz, seq, ch), BF16),
        scratch_shapes=[pltpu.VMEM((CONV_HALO + tt, ch), F32)],
        compiler_params=_cparams("parallel", "arbitrary"),
        name="conv_module",
    )(u, state_pad, w_pad, b, ln_g, ln_b)


def _fox_attn_kernel(q_ref, k_ref, v_ref, cc_ref, cr_ref, o_ref, m_ref, l_ref, acc_ref, *, tq):
    qi = pl.program_id(2)
    m_ref[...] = jnp.full_like(m_ref, NEG)
    l_ref[...] = jnp.zeros_like(l_ref)
    acc_ref[...] = jnp.zeros_like(acc_ref)
    cc = cc_ref[0, 0]

    def block(j, masked):
        ks = pl.multiple_of(j * tq, tq)
        for hh in range(2):
            sl = slice(hh * HEAD_DIM, (hh + 1) * HEAD_DIM)
            s = _dot_nt(q_ref[0, :, sl], k_ref[0, pl.ds(ks, tq), sl])
            s = s + (cc[:, hh:hh + 1] - cr_ref[0, 0, hh:hh + 1, pl.ds(ks, tq)])
            if masked:
                row = lax.broadcasted_iota(jnp.int32, (tq, tq), 0)
                col = lax.broadcasted_iota(jnp.int32, (tq, tq), 1)
                s = jnp.where(col <= row, s, NEG)
            m_prev = m_ref[hh]
            m_new = jnp.maximum(m_prev, jnp.max(s, axis=-1, keepdims=True))
            p = jnp.exp(s - m_new)
            alpha = jnp.exp(m_prev - m_new)
            l_ref[hh] = alpha * l_ref[hh] + jnp.sum(p, axis=-1, keepdims=True)
            acc_ref[hh] = alpha * acc_ref[hh] + _dot(p.astype(BF16), v_ref[0, pl.ds(ks, tq), sl])
            m_ref[hh] = m_new

    def body(j, carry):
        block(j, False)
        return carry

    lax.fori_loop(0, qi, body, 0)
    block(qi, True)
    o_ref[0] = jnp.concatenate([acc_ref[hh] / l_ref[hh] for hh in range(2)], axis=-1).astype(o_ref.dtype)


def _fox_attention(q, k, v, c_col, c_row, *, tq):
    bsz, seq, width = q.shape
    pairs = width // (2 * HEAD_DIM)
    return pl.pallas_call(
        functools.partial(_fox_attn_kernel, tq=tq),
        grid=(bsz, pairs, seq // tq),
        in_specs=[pl.BlockSpec((1, tq, LANES), lambda b, h, i: (b, i, h)),
                  pl.BlockSpec((1, seq, LANES), lambda b, h, i: (b, 0, h)),
                  pl.BlockSpec((1, seq, LANES), lambda b, h, i: (b, 0, h)),
                  pl.BlockSpec((1, 1, tq, 2), lambda b, h, i: (b, h, i, 0)),
                  pl.BlockSpec((1, 1, 2, seq), lambda b, h, i: (b, h, 0, 0))],
        out_specs=pl.BlockSpec((1, tq, LANES), lambda b, h, i: (b, i, h)),
        out_shape=jax.ShapeDtypeStruct((bsz, seq, width), BF16),
        scratch_shapes=[pltpu.VMEM((2, tq, 1), F32), pltpu.VMEM((2, tq, 1), F32),
                        pltpu.VMEM((2, tq, HEAD_DIM), F32)],
        compiler_params=_cparams("parallel", "parallel", "arbitrary"),
        name="fox_attention",
    )(q, k, v, c_col, c_row)


def _later_key_matrix(n, dtype):
    row = lax.broadcasted_iota(jnp.int32, (n, n), 0)
    col = lax.broadcasted_iota(jnp.int32, (n, n), 1)
    return jnp.where(row > col, 1.0, 0.0).astype(dtype)


def _sb_block(q, k, v, later, r_prev, valid, transposed=False):
    z = _dot(q, k) if transposed else _dot_nt(q, k)
    log_remain = -_softplus(z)
    log_beta = z + log_remain
    if valid is not None:
        log_remain = jnp.where(valid, log_remain, 0.0)
    stick = _split_dot(log_remain, later) + r_prev
    w = jnp.exp(log_beta + stick)
    if valid is not None:
        w = jnp.where(valid, w, 0.0)
    r_new = stick[:, 0:1] + log_remain[:, 0:1]
    w = w.astype(BF16)
    return (_dot_nt(w, v) if transposed else _dot(w, v)), r_new


def _sb_attn_kernel(q_ref, k_ref, v_ref, o_ref, r_ref, acc_ref, *, tq):
    qi = pl.program_id(2)
    later = _later_key_matrix(tq, BF16)
    row = lax.broadcasted_iota(jnp.int32, (tq, tq), 0)
    col = lax.broadcasted_iota(jnp.int32, (tq, tq), 1)
    ks = pl.multiple_of(qi * tq, tq)
    for hh in range(2):
        sl = slice(hh * HEAD_DIM, (hh + 1) * HEAD_DIM)
        o, r = _sb_block(q_ref[0, :, sl], k_ref[0, pl.ds(ks, tq), sl], v_ref[0, pl.ds(ks, tq), sl],
                         later, 0.0, col < row)
        acc_ref[hh] = o
        r_ref[hh] = r

    def r_max():
        return jnp.max(jnp.maximum(r_ref[0], r_ref[1]))

    def more(carry):
        jj, r_top = carry
        return jnp.logical_and(jj < qi, r_top > SB_DEAD)

    def body(carry):
        jj, _ = carry
        ks_ = pl.multiple_of((qi - 1 - jj) * tq, tq)
        for hh in range(2):
            sl = slice(hh * HEAD_DIM, (hh + 1) * HEAD_DIM)
            o, r = _sb_block(q_ref[0, :, sl], k_ref[0, pl.ds(ks_, tq), sl], v_ref[0, pl.ds(ks_, tq), sl],
                             later, r_ref[hh], None)
            acc_ref[hh] = acc_ref[hh] + o
            r_ref[hh] = r
        return jj + 1, r_max()

    lax.while_loop(more, body, (jnp.int32(0), r_max()))
    o_ref[0] = jnp.concatenate([acc_ref[hh] for hh in range(2)], axis=-1).astype(o_ref.dtype)


def _sb_attention(q, k, v, *, tq):
    bsz, seq, width = q.shape
    pairs = width // (2 * HEAD_DIM)
    return pl.pallas_call(
        functools.partial(_sb_attn_kernel, tq=tq),
        grid=(bsz, pairs, seq // tq),
        in_specs=[pl.BlockSpec((1, tq, LANES), lambda b, h, i: (b, i, h)),
                  pl.BlockSpec((1, seq, LANES), lambda b, h, i: (b, 0, h)),
                  pl.BlockSpec((1, seq, LANES), lambda b, h, i: (b, 0, h))],
        out_specs=pl.BlockSpec((1, tq, LANES), lambda b, h, i: (b, i, h)),
        out_shape=jax.ShapeDtypeStruct((bsz, seq, width), BF16),
        scratch_shapes=[pltpu.VMEM((2, tq, 1), F32), pltpu.VMEM((2, tq, HEAD_DIM), F32)],
        compiler_params=_cparams("parallel", "parallel", "arbitrary"),
        name="sb_attention",
    )(q, k, v)


def _mem_attn_kernel(q_ref, k_ref, v_ref, o_ref):
    outs = []
    for h in range(MEM_HEADS):
        sl = slice(h * MEM_HEAD_DIM, (h + 1) * MEM_HEAD_DIM)
        s = _dot_nt(q_ref[0, :, sl].astype(BF16), k_ref[0, :, sl].astype(BF16))
        p = jnp.exp(s - jnp.max(s, axis=-1, keepdims=True))
        o = _dot(p.astype(BF16), v_ref[0, :, sl].astype(BF16))
        outs.append(o / jnp.sum(p, axis=-1, keepdims=True))
    o_ref[0] = jnp.concatenate(outs, axis=-1).astype(o_ref.dtype)


def _mem_attention(q, mk, mv, *, tm):
    bsz, seq, width = q.shape
    n_mem = mk.shape[1]
    return pl.pallas_call(
        _mem_attn_kernel,
        grid=(bsz, seq // tm),
        in_specs=[pl.BlockSpec((1, tm, width), lambda b, i: (b, i, 0)),
                  pl.BlockSpec((1, n_mem, width), lambda b, i: (b, 0, 0)),
                  pl.BlockSpec((1, n_mem, width), lambda b, i: (b, 0, 0))],
        out_specs=pl.BlockSpec((1, tm, width), lambda b, i: (b, i, 0)),
        out_shape=jax.ShapeDtypeStruct((bsz, seq, width), BF16),
        compiler_params=_cparams("parallel", "parallel"),
        name="mem_attention",
    )(q, mk, mv)


def _ffn_kernel(*refs, seq, tiles_per_seq):
    (x_ref, g4_ref, g5_ref, wg_ref, wv_ref, cwg_ref, cwv_ref, cbg_ref, cbv_ref, wd_ref) = refs[:10]
    if tiles_per_seq:
        sg_ref, sv_ref, o_ref, og_ref, ov_ref, h_ref, acc_ref, cg_ref, cv_ref = refs[10:]
    else:
        s1g_ref, s1v_ref, s2g_ref, s2v_ref, o_ref, og_ref, ov_ref, h_ref, acc_ref = refs[10:]
    i, j = pl.program_id(0), pl.program_id(1)
    tm = x_ref.shape[0]
    tf = wg_ref.shape[1]

    @pl.when(j == 0)
    def _():
        h_ref[...] = _rms(x_ref[...], g4_ref[...]).astype(BF16)
        acc_ref[...] = jnp.zeros_like(acc_ref)

    h = h_ref[...]
    rmod = lax.broadcasted_iota(jnp.int32, (tm, tf), 0) % seq

    def conv(u, w_ref, b_ref, prev1, prev2):
        s1 = jnp.where(rmod >= 1, pltpu.roll(u, 1, 0), prev1)
        s2 = jnp.where(rmod >= 2, pltpu.roll(u, 2, 0), prev2)
        return s2 * w_ref[0:1, :] + s1 * w_ref[1:2, :] + u * w_ref[2:3, :] + b_ref[...]

    ug = _dot(h, wg_ref[...])
    uv = _dot(h, wv_ref[...])
    if tiles_per_seq:
        @pl.when(i % tiles_per_seq == 0)
        def _():
            cg_ref[j] = sg_ref[0]
            cv_ref[j] = sv_ref[0]

        pg = cg_ref[j]
        pv = cv_ref[j]
        cg_ref[j] = ug[tm - 2:tm, :]
        cv_ref[j] = uv[tm - 2:tm, :]
        og_ref[0] = ug[tm - 2:tm, :]
        ov_ref[0] = uv[tm - 2:tm, :]
        cgate = conv(ug, cwg_ref, cbg_ref, pg[1:2, :], jnp.where(rmod == 0, pg[0:1, :], pg[1:2, :]))
        cval = conv(uv, cwv_ref, cbv_ref, pv[1:2, :], jnp.where(rmod == 0, pv[0:1, :], pv[1:2, :]))
    else:
        og_ref[...] = ug
        ov_ref[...] = uv
        cgate = conv(ug, cwg_ref, cbg_ref, s1g_ref[...], s2g_ref[...])
        cval = conv(uv, cwv_ref, cbv_ref, s1v_ref[...], s2v_ref[...])
    act = (cgate * _sigmoid(cgate) * cval).astype(BF16)
    acc_ref[...] += _dot(act, wd_ref[...])

    @pl.when(j == pl.num_programs(1) - 1)
    def _():
        o_ref[...] = x_ref[...] + _rms(acc_ref[...], g5_ref[...])


def _conv_ffn(x, g4, g5, w_up, conv_w, conv_b, w_down, state, *, bsz, seq, tm, tf):
    m, d = x.shape
    d_ff = w_down.shape[0]
    nf = d_ff // tf
    tiles_per_seq = seq // tm if seq % tm == 0 else 0
    kw = FFN_CONV_WIDTH
    in_specs = [pl.BlockSpec((tm, d), lambda i, j: (i, 0)), _full((1, d)), _full((1, d)),
                pl.BlockSpec((d, tf), lambda i, j: (0, j)),
                pl.BlockSpec((d, tf), lambda i, j: (0, nf + j)),
                pl.BlockSpec((kw, tf), lambda i, j: (0, j)),
                pl.BlockSpec((kw, tf), lambda i, j: (0, nf + j)),
                pl.BlockSpec((1, tf), lambda i, j: (0, j)),
                pl.BlockSpec((1, tf), lambda i, j: (0, nf + j)),
                pl.BlockSpec((tf, d), lambda i, j: (j, 0))]
    args = [x, g4, g5, w_up, w_up, conv_w, conv_w, conv_b, conv_b, w_down]
    scratch = [pltpu.VMEM((tm, d), BF16), pltpu.VMEM((tm, d), F32)]
    x_spec = pl.BlockSpec((tm, d), lambda i, j: (i, 0))
    if tiles_per_seq:
        in_specs += [pl.BlockSpec((1, kw - 1, tf), lambda i, j: (i // tiles_per_seq, 0, j)),
                     pl.BlockSpec((1, kw - 1, tf), lambda i, j: (i // tiles_per_seq, 0, nf + j))]
        args += [state, state]
        st_spec = pl.BlockSpec((1, kw - 1, tf), lambda i, j: (i, 0, j))
        out_specs = [x_spec, st_spec, st_spec]
        out_shape = [jax.ShapeDtypeStruct((m, d), F32),
                     jax.ShapeDtypeStruct((m // tm, kw - 1, d_ff), F32),
                     jax.ShapeDtypeStruct((m // tm, kw - 1, d_ff), F32)]
        scratch += [pltpu.VMEM((nf, kw - 1, tf), F32), pltpu.VMEM((nf, kw - 1, tf), F32)]
    else:
        assert tm % seq == 0 and seq >= kw - 1
        pad = seq - (kw - 1)
        s2 = jnp.pad(state, ((0, 0), (0, pad), (0, 0))).reshape(m, 2 * d_ff)
        s1 = jnp.pad(state[:, 1:], ((0, 0), (0, seq - 1), (0, 0))).reshape(m, 2 * d_ff)
        in_specs += [pl.BlockSpec((tm, tf), lambda i, j: (i, j)),
                     pl.BlockSpec((tm, tf), lambda i, j: (i, nf + j)),
                     pl.BlockSpec((tm, tf), lambda i, j: (i, j)),
                     pl.BlockSpec((tm, tf), lambda i, j: (i, nf + j))]
        args += [s1, s1, s2, s2]
        u_spec = pl.BlockSpec((tm, tf), lambda i, j: (i, j))
        out_specs = [x_spec, u_spec, u_spec]
        out_shape = [jax.ShapeDtypeStruct((m, d), F32), jax.ShapeDtypeStruct((m, d_ff), F32),
                     jax.ShapeDtypeStruct((m, d_ff), F32)]
    y, og, ov = pl.pallas_call(
        functools.partial(_ffn_kernel, seq=seq, tiles_per_seq=tiles_per_seq),
        grid=(m // tm, nf),
        in_specs=in_specs,
        out_specs=out_specs,
        out_shape=out_shape,
        scratch_shapes=scratch,
        compiler_params=_cparams("arbitrary", "arbitrary"),
        name="conv_ffn",
    )(*args)
    if tiles_per_seq:
        og = og[tiles_per_seq - 1::tiles_per_seq]
        ov = ov[tiles_per_seq - 1::tiles_per_seq]
    else:
        og = og.reshape(bsz, seq, d_ff)[:, seq - (kw - 1):]
        ov = ov.reshape(bsz, seq, d_ff)[:, seq - (kw - 1):]
    return y, jnp.concatenate([og, ov], axis=-1)


def _block_diag_queries(q, n_heads):
    t, width = q.shape
    tiled = jnp.concatenate([q.astype(F32)] * n_heads, axis=0)
    row = lax.broadcasted_iota(jnp.int32, (n_heads * t, width), 0)
    col = lax.broadcasted_iota(jnp.int32, (n_heads * t, width), 1)
    return jnp.where(row // t == col // HEAD_DIM, tiled, 0.0).astype(BF16)


def _diag_blocks(acc, t, n_heads):
    return jnp.concatenate(
        [acc[h * t:(h + 1) * t, h * HEAD_DIM:(h + 1) * HEAD_DIM] for h in range(n_heads)], axis=-1)


def _fox_decode_kernel(pt_ref, q_ref, kn_ref, vn_ref, lq_ref, *refs, pages_per_step, n_heads, t_new):
    npg = pages_per_step
    k_refs, v_refs, lf_refs = refs[:npg], refs[npg:2 * npg], refs[2 * npg:3 * npg]
    o_ref, qbd_ref, m_ref, l_ref, acc_ref, carry_ref, gq_ref = refs[3 * npg:]
    step = pl.program_id(1)
    rows = n_heads * t_new
    later = _later_key_matrix(LANES, F32)
    lane = lax.broadcasted_iota(jnp.int32, (rows, LANES), 1)
    qpos = lax.broadcasted_iota(jnp.int32, (rows, LANES), 0) % t_new

    def update(s, v):
        m_prev = m_ref[...]
        m_new = jnp.maximum(m_prev, jnp.max(s, axis=-1, keepdims=True))
        p = jnp.exp(s - m_new)
        alpha = jnp.exp(m_prev - m_new)
        l_ref[...] = alpha * l_ref[...] + jnp.sum(p, axis=-1, keepdims=True)
        acc_ref[...] = alpha * acc_ref[...] + _dot_nt(p.astype(BF16), v)
        m_ref[...] = m_new

    @pl.when(step == 0)
    def _():
        qbd_ref[...] = _block_diag_queries(q_ref[0], n_heads)
        m_ref[...] = jnp.full_like(m_ref, NEG)
        l_ref[...] = jnp.zeros_like(l_ref)
        acc_ref[...] = jnp.zeros_like(acc_ref)
        logf_rows = lq_ref[0]
        gap = _dot_exact(logf_rows, later)
        carry_ref[...] = gap[:, 0:1] + logf_rows[:, 0:1]
        gq_ref[...] = jnp.sum(jnp.where(lane == qpos, gap, 0.0), axis=-1, keepdims=True)
        s = (gap - gq_ref[...]) + _dot(qbd_ref[...], kn_ref[0])
        update(jnp.where(lane <= qpos, s, NEG), vn_ref[0])

    for p_ in range(npg):
        lf_t = lf_refs[p_][0]
        logf_rows = jnp.concatenate(
            [jnp.broadcast_to(lf_t[h:h + 1, :], (t_new, LANES)) for h in range(n_heads)], axis=0)
        gap = _dot_exact(logf_rows, later) + carry_ref[...]
        carry_ref[...] = gap[:, 0:1] + logf_rows[:, 0:1]
        s = (gap - gq_ref[...]) + _dot(qbd_ref[...], k_refs[p_][0].astype(BF16))
        update(s, v_refs[p_][0].astype(BF16))

    @pl.when(step == pl.num_programs(1) - 1)
    def _():
        o_ref[0] = _diag_blocks(acc_ref[...] / l_ref[...], t_new, n_heads).astype(o_ref.dtype)


def _sb_decode_kernel(pt_ref, q_ref, kn_ref, vn_ref, *refs, pages_per_step, n_heads, t_new):
    npg = pages_per_step
    k_refs, v_refs = refs[:npg], refs[npg:2 * npg]
    o_ref, qbd_ref, r_ref, acc_ref = refs[2 * npg:]
    step = pl.program_id(1)
    rows = n_heads * t_new
    later = _later_key_matrix(LANES, BF16)

    @pl.when(step == 0)
    def _():
        qbd_ref[...] = _block_diag_queries(q_ref[0], n_heads)
        lane = lax.broadcasted_iota(jnp.int32, (rows, LANES), 1)
        qpos = lax.broadcasted_iota(jnp.int32, (rows, LANES), 0) % t_new
        o, r = _sb_block(qbd_ref[...], kn_ref[0], vn_ref[0], later, 0.0, lane < qpos, transposed=True)
        acc_ref[...] = o
        r_ref[...] = r

    for p_ in range(npg):
        o, r = _sb_block(qbd_ref[...], k_refs[p_][0].astype(BF16), v_refs[p_][0].astype(BF16),
                         later, r_ref[...], None, transposed=True)
        acc_ref[...] += o
        r_ref[...] = r

    @pl.when(step == pl.num_programs(1) - 1)
    def _():
        o_ref[0] = _diag_blocks(acc_ref[...], t_new, n_heads).astype(o_ref.dtype)


def _decode_attention(kind, page_table, q, k_new, v_new, k_pool, v_pool, logf_new=None, logf_pool=None,
                      *, pages_per_step):
    bsz, t_new, width = q.shape
    n_heads = width // HEAD_DIM
    n_pages = page_table.shape[1]
    page = k_pool.shape[2]
    assert page == LANES and t_new <= page and n_pages % pages_per_step == 0
    n_steps = n_pages // pages_per_step
    rows = n_heads * t_new
    pad = ((0, 0), (0, 0), (0, page - t_new))
    kn, vn = jnp.pad(jnp.swapaxes(k_new, 1, 2), pad), jnp.pad(jnp.swapaxes(v_new, 1, 2), pad)

    def page_spec(shape, p_):
        def index(b, s, pt):
            return (pt[b, n_pages - 1 - (s * pages_per_step + p_)], 0, 0)
        return pl.BlockSpec(shape, index)

    per_seq = lambda shape: pl.BlockSpec(shape, lambda b, s, pt: (b, 0, 0))
    in_specs = [per_seq((1, t_new, width)), per_seq((1, width, page)), per_seq((1, width, page))]
    args = [q, kn, vn]
    kv_specs = [page_spec((1, width, page), p_) for p_ in range(pages_per_step)]
    if kind == "fox":
        lq = jnp.repeat(jnp.swapaxes(logf_new, 1, 2), t_new, axis=1)
        lq = jnp.pad(lq, ((0, 0), (0, 0), (0, page - t_new)))
        lf_t = jnp.swapaxes(logf_pool, 1, 2)
        in_specs += [per_seq((1, rows, page))] + kv_specs + kv_specs
        in_specs += [page_spec((1, n_heads, page), p_) for p_ in range(pages_per_step)]
        args += [lq] + [k_pool] * pages_per_step + [v_pool] * pages_per_step + [lf_t] * pages_per_step
        body = _fox_decode_kernel
        scratch = [pltpu.VMEM((rows, width), BF16), pltpu.VMEM((rows, 1), F32), pltpu.VMEM((rows, 1), F32),
                   pltpu.VMEM((rows, width), F32), pltpu.VMEM((rows, 1), F32), pltpu.VMEM((rows, 1), F32)]
    else:
        in_specs += kv_specs + kv_specs
        args += [k_pool] * pages_per_step + [v_pool] * pages_per_step
        body = _sb_decode_kernel
        scratch = [pltpu.VMEM((rows, width), BF16), pltpu.VMEM((rows, 1), F32),
                   pltpu.VMEM((rows, width), F32)]
    return pl.pallas_call(
        functools.partial(body, pages_per_step=pages_per_step, n_heads=n_heads, t_new=t_new),
        grid_spec=pltpu.PrefetchScalarGridSpec(
            num_scalar_prefetch=1,
            grid=(bsz, n_steps),
            in_specs=in_specs,
            out_specs=pl.BlockSpec((1, t_new, width), lambda b, s, pt: (b, 0, 0)),
            scratch_shapes=scratch),
        out_shape=jax.ShapeDtypeStruct((bsz, t_new, width), F32),
        compiler_params=_cparams("parallel", "arbitrary"),
        name=kind + "_decode_attention",
    )(page_table, *args)


def _prep_weights(p):
    w = {}
    n_even = p["even_w_in"].shape[0]
    ch = p["conv_w"].shape[2]
    even_in = p["even_w_in"].shape[2]
    n_fox = even_in - 5 * ch
    w["even_w_in"] = jnp.pad(p["even_w_in"], ((0, 0), (0, 0), (0, LANES - n_fox))).astype(BF16)
    w["even_b_f"] = jnp.pad(p["even_b_f"], ((0, 0), (0, LANES - n_fox))).reshape(n_even, 1, LANES)
    w["conv_w"] = jnp.pad(p["conv_w"], ((0, 0), (0, CONV_HALO - CONV_WIDTH), (0, 0)))
    for name in ("even_w_out", "sb_w_in", "sb_w_out", "mem_wq", "mem_wo", "ffn_w_up", "ffn_w_down"):
        w[name] = p[name].astype(BF16)
    w["mem_wkv"] = jnp.concatenate([p["mem_wk"], p["mem_wv"]], axis=-1).astype(BF16)
    return w


def _pages_transposed(pool):
    n, page, heads, dim = pool.shape
    return jnp.transpose(pool, (0, 2, 3, 1)).reshape(n, heads * dim, page)


def _run_trunk(x, mem_k, mem_v, conv_state, ffn_state, fox_cache, sb_cache, page_table, p, w):
    bsz, seq, d = x.shape
    m = bsz * seq
    depth = p["norm_g"].shape[0]
    ch = p["conv_w"].shape[2]
    n_fox = p["even_b_f"].shape[1]
    prompt = fox_cache is None
    tm = min(m, 256)
    tq = min(seq, 256)
    x = x.reshape(m, d)
    fox_k, fox_v, fox_logf, conv_new, sb_k, sb_v, ffn_new = [], [], [], [], [], [], []
    for layer in range(depth):
        g = p["norm_g"][layer].reshape(6, 1, d)
        i = layer // 2
        if layer % 2 == 0:
            u, q, k, v, kb, vb, logf, cum = _even_proj(
                x, g[0], w["even_w_in"][i], w["even_b_f"][i], ch=ch, nh=n_fox, seq=seq, tm=tm)
            u3 = u.reshape(bsz, seq, ch)
            state = conv_state[i]
            halo = CONV_WIDTH - 1
            state_pad = jnp.pad(state, ((0, 0), (CONV_HALO - halo, 0), (0, 0)))
            a_out = _conv_module(u3, state_pad, w["conv_w"][i], p["conv_b"][i].reshape(1, ch),
                                 p["conv_ln_g"][i].reshape(1, ch), p["conv_ln_b"][i].reshape(1, ch))
            conv_new.append(jnp.concatenate([state, u3], axis=1)[:, seq:])
            q3, kb3, vb3 = (a.reshape(bsz, seq, ch) for a in (q, kb, vb))
            if prompt:
                c4 = cum.reshape(bsz, seq, n_fox // 2, 2)
                o = _fox_attention(q3, kb3, vb3, jnp.transpose(c4, (0, 2, 1, 3)),
                                   jnp.transpose(c4, (0, 2, 3, 1)), tq=tq)
            else:
                pool = lambda c: _pages_transposed(c[i])
                o = _decode_attention("fox", page_table, q3.astype(F32), kb3, vb3, pool(fox_cache[0]), pool(fox_cache[1]),
                                      logf.reshape(bsz, seq, n_fox), fox_cache[2][i], pages_per_step=4)
            x = _linear_residual([a_out.reshape(m, ch), o.reshape(m, ch).astype(BF16)],
                                 [w["even_w_out"][i][:ch], w["even_w_out"][i][ch:]], x, g[1], tm=tm)
            fox_k.append(k.reshape(bsz, seq, n_fox, HEAD_DIM))
            fox_v.append(v.reshape(bsz, seq, n_fox, HEAD_DIM))
            fox_logf.append(logf.reshape(bsz, seq, n_fox))
        else:
            q, k, v, kb, vb = _odd_proj(x, g[0], w["sb_w_in"][i], tm=tm)
            width = q.shape[1]
            q3, kb3, vb3 = (a.reshape(bsz, seq, width) for a in (q, kb, vb))
            if prompt:
                o = _sb_attention(q3, kb3, vb3, tq=tq)
            else:
                pool = lambda c: _pages_transposed(c[i])
                o = _decode_attention("sb", page_table, q3.astype(F32), kb3, vb3, pool(sb_cache[0]), pool(sb_cache[1]),
                                      pages_per_step=4)
            x = _linear_residual([o.reshape(m, width).astype(BF16)], [w["sb_w_out"][i]], x, g[1], tm=tm)
            sb_k.append(k.reshape(bsz, seq, width // HEAD_DIM, HEAD_DIM))
            sb_v.append(v.reshape(bsz, seq, width // HEAD_DIM, HEAD_DIM))
        qm = _norm_matmul(x, g[2], w["mem_wq"][layer], tm=tm, out_dtype=BF16 if seq % 16 == 0 else F32,
                          scale=MEM_HEAD_DIM ** -0.5)
        mw = qm.shape[1]
        om = _mem_attention(qm.reshape(bsz, seq, mw), mem_k[layer], mem_v[layer], tm=min(seq, 512))
        x = _linear_residual([om.reshape(m, mw)], [w["mem_wo"][layer]], x, g[3], tm=tm)
        d_ff2 = p["ffn_w_up"].shape[2]
        x, buf = _conv_ffn(x, g[4], g[5], w["ffn_w_up"][layer], p["ffn_conv_w"][layer],
                           p["ffn_conv_b"][layer].reshape(1, d_ff2), w["ffn_w_down"][layer],
                           ffn_state[layer], bsz=bsz, seq=seq, tm=min(m, 512), tf=256)
        ffn_new.append(buf)
    return (x.reshape(bsz, seq, d), jnp.stack(fox_k), jnp.stack(fox_v), jnp.stack(fox_logf),
            jnp.stack(conv_new), jnp.stack(sb_k), jnp.stack(sb_v), jnp.stack(ffn_new))


def kernel(x_prompt, x_sample, cache_fox_k, cache_fox_v, cache_fox_logf, state_conv, cache_sb_k, cache_sb_v, cache_mem_k, cache_mem_v, state_ffn_conv, page_table, mem_prompt, norm_g, even_w_in, even_b_f, conv_w, conv_b, conv_ln_g, conv_ln_b, even_w_out, sb_w_in, sb_w_out, mem_norm_g, mem_wq, mem_wk, mem_wv, mem_wo, ffn_w_up, ffn_conv_w, ffn_conv_b, ffn_w_down):
    p = dict(norm_g=norm_g, even_w_in=even_w_in, even_b_f=even_b_f, conv_w=conv_w, conv_b=conv_b,
             conv_ln_g=conv_ln_g, conv_ln_b=conv_ln_b, even_w_out=even_w_out, sb_w_in=sb_w_in,
             sb_w_out=sb_w_out, mem_wq=mem_wq, mem_wk=mem_wk, mem_wv=mem_wv, mem_wo=mem_wo,
             ffn_w_up=ffn_w_up, ffn_conv_w=ffn_conv_w, ffn_conv_b=ffn_conv_b, ffn_w_down=ffn_w_down)
    w = _prep_weights(p)
    depth, d = norm_g.shape[0], norm_g.shape[2]
    bsz, n_mem = mem_prompt.shape[0], mem_prompt.shape[1]
    mem_width = mem_wq.shape[2]
    n_even = even_w_in.shape[0]
    ch = conv_w.shape[2]

    mem_flat = mem_prompt.reshape(bsz * n_mem, d)
    mem_kv = [_norm_matmul(mem_flat, mem_norm_g[l].reshape(1, d), w["mem_wkv"][l], tm=256, out_dtype=F32)
              for l in range(depth)]
    mem_k_prompt = jnp.stack([kv[:, :mem_width].reshape(bsz, n_mem, mem_width) for kv in mem_kv])
    mem_v_prompt = jnp.stack([kv[:, mem_width:].reshape(bsz, n_mem, mem_width) for kv in mem_kv])
    conv_zero = jnp.zeros((n_even, bsz, CONV_WIDTH - 1, ch), F32)
    ffn_zero = jnp.zeros((depth, bsz, FFN_CONV_WIDTH - 1, ffn_w_up.shape[2]), F32)
    (y_prompt, fox_k_prompt, fox_v_prompt, fox_logf_prompt, conv_state_prompt,
     sb_k_prompt, sb_v_prompt, ffn_state_prompt) = _run_trunk(
        x_prompt, mem_k_prompt, mem_v_prompt, conv_zero, ffn_zero, None, None, None, p, w)

    dbsz = x_sample.shape[0]
    cmk = cache_mem_k.reshape(depth, dbsz, n_mem, mem_width)
    cmv = cache_mem_v.reshape(depth, dbsz, n_mem, mem_width)
    (y_sample, fox_k_sample, fox_v_sample, fox_logf_sample, conv_state_sample,
     sb_k_sample, sb_v_sample, ffn_state_sample) = _run_trunk(
        x_sample, cmk, cmv, state_conv, state_ffn_conv,
        (cache_fox_k, cache_fox_v, cache_fox_logf), (cache_sb_k, cache_sb_v), page_table, p, w)

    mem_shape = (depth, bsz, n_mem, MEM_HEADS, MEM_HEAD_DIM)
    return (y_prompt, y_sample, fox_k_prompt, fox_v_prompt, fox_logf_prompt, conv_state_prompt,
            sb_k_prompt, sb_v_prompt, mem_k_prompt.reshape(mem_shape), mem_v_prompt.reshape(mem_shape),
            ffn_state_prompt, fox_k_sample, fox_v_sample, fox_logf_sample, conv_state_sample,
            sb_k_sample, sb_v_sample, ffn_state_sample)
```

```python
import functools

import jax
import jax.numpy as jnp
from jax import lax
from jax.experimental import pallas as pl
from jax.experimental.pallas import tpu as pltpu

F32 = jnp.float32
BF16 = jnp.bfloat16

EPS = 1e-6
NEG = -1e30
SB_DEAD = -105.0
HEAD_DIM = 64
MEM_HEADS = 4
MEM_HEAD_DIM = 128
CONV_WIDTH = 31
FFN_CONV_WIDTH = 3
LANES = 128
SUBLANES = 8
CONV_HALO = 32
VMEM_LIMIT = 48 * 1024 * 1024


def _cparams(*sem):
    return pltpu.CompilerParams(dimension_semantics=sem, vmem_limit_bytes=VMEM_LIMIT)


def _dot(a, b):
    return jnp.dot(a, b, preferred_element_type=F32)


def _dot_nt(a, b):
    return lax.dot_general(a, b, (((1,), (1,)), ((), ())), preferred_element_type=F32)


def _dot_exact(a, b):
    return jnp.dot(a, b, preferred_element_type=F32, precision=lax.Precision.HIGHEST)


def _rms(x, g):
    return x * lax.rsqrt(jnp.mean(x * x, axis=-1, keepdims=True) + EPS) * g


def _softplus(z):
    return jnp.maximum(z, 0.0) + jnp.log(1.0 + jnp.exp(-jnp.abs(z)))


def _sigmoid(z):
    return 1.0 / (1.0 + jnp.exp(-z))


def _split_dot(x, u):
    hi = x.astype(BF16)
    lo = (x - hi.astype(F32)).astype(BF16)
    return _dot(hi, u) + _dot(lo, u)


def _full(shape):
    nd = len(shape)
    return pl.BlockSpec(shape, lambda *_: (0,) * nd)


def _even_proj_kernel(x_ref, g_ref, w_ref, bf_ref, u_ref, q_ref, k_ref, v_ref, kb_ref, vb_ref,
                      lf_ref, c_ref, carry_ref, *, ch, tiles_per_seq):
    tm = x_ref.shape[0]
    h = _rms(x_ref[...], g_ref[...]).astype(BF16)
    ag = _dot(h, w_ref[:, 0:2 * ch])
    u_ref[...] = ag[:, :ch] * _sigmoid(ag[:, ch:])
    q_ref[...] = (_dot(h, w_ref[:, 2 * ch:3 * ch]) * HEAD_DIM ** -0.5).astype(BF16)
    k = _dot(h, w_ref[:, 3 * ch:4 * ch])
    k_ref[...] = k
    kb_ref[...] = k.astype(BF16)
    v = _dot(h, w_ref[:, 4 * ch:5 * ch])
    v_ref[...] = v
    vb_ref[...] = v.astype(BF16)
    f = _dot(h, w_ref[:, 5 * ch:5 * ch + LANES]) + bf_ref[...]
    lf = -_softplus(-f)
    nh = lf_ref.shape[1]
    lf_ref[...] = lf[:, :nh]
    if tiles_per_seq:
        @pl.when(pl.program_id(0) % tiles_per_seq == 0)
        def _():
            carry_ref[...] = jnp.zeros_like(carry_ref)
        row = lax.broadcasted_iota(jnp.int32, (tm, tm), 0)
        col = lax.broadcasted_iota(jnp.int32, (tm, tm), 1)
        tri = jnp.where(row >= col, 1.0, 0.0).astype(F32)
        c = _dot_exact(tri, lf) + carry_ref[...]
        carry_ref[...] = c[tm - 1:tm, :]
        c_ref[...] = c[:, :nh]
    else:
        c_ref[...] = lf[:, :nh]


def _even_proj(x, g, w_pad, bf_pad, *, ch, nh, seq, tm):
    m, d = x.shape
    tiles_per_seq = seq // tm if seq % tm == 0 else 0
    row = lambda n: pl.BlockSpec((tm, n), lambda i: (i, 0))
    outs = [jax.ShapeDtypeStruct((m, ch), F32), jax.ShapeDtypeStruct((m, ch), BF16),
            jax.ShapeDtypeStruct((m, ch), F32), jax.ShapeDtypeStruct((m, ch), F32),
            jax.ShapeDtypeStruct((m, ch), BF16), jax.ShapeDtypeStruct((m, ch), BF16),
            jax.ShapeDtypeStruct((m, nh), F32), jax.ShapeDtypeStruct((m, nh), F32)]
    return pl.pallas_call(
        functools.partial(_even_proj_kernel, ch=ch, tiles_per_seq=tiles_per_seq),
        grid=(m // tm,),
        in_specs=[row(d), _full((1, d)), _full(w_pad.shape), _full((1, LANES))],
        out_specs=[row(ch)] * 6 + [row(nh)] * 2,
        out_shape=outs,
        scratch_shapes=[pltpu.VMEM((1, LANES), F32)],
        compiler_params=_cparams("arbitrary"),
        name="even_proj",
    )(x, g, w_pad, bf_pad)


def _odd_proj_kernel(x_ref, g_ref, w_ref, q_ref, k_ref, v_ref, kb_ref, vb_ref, *, width):
    h = _rms(x_ref[...], g_ref[...]).astype(BF16)
    q_ref[...] = (_dot(h, w_ref[:, 0:width]) * HEAD_DIM ** -0.5).astype(BF16)
    k = _dot(h, w_ref[:, width:2 * width])
    k_ref[...] = k
    kb_ref[...] = k.astype(BF16)
    v = _dot(h, w_ref[:, 2 * width:3 * width])
    v_ref[...] = v
    vb_ref[...] = v.astype(BF16)


def _odd_proj(x, g, w, *, tm):
    m, d = x.shape
    width = w.shape[1] // 3
    row = lambda n: pl.BlockSpec((tm, n), lambda i: (i, 0))
    outs = [jax.ShapeDtypeStruct((m, width), BF16), jax.ShapeDtypeStruct((m, width), F32),
            jax.ShapeDtypeStruct((m, width), F32), jax.ShapeDtypeStruct((m, width), BF16),
            jax.ShapeDtypeStruct((m, width), BF16)]
    return pl.pallas_call(
        functools.partial(_odd_proj_kernel, width=width),
        grid=(m // tm,),
        in_specs=[row(d), _full((1, d)), _full(w.shape)],
        out_specs=[row(width)] * 5,
        out_shape=outs,
        compiler_params=_cparams("parallel"),
        name="odd_proj",
    )(x, g, w)


def _norm_matmul_kernel(x_ref, g_ref, w_ref, o_ref, *, scale):
    h = _rms(x_ref[...], g_ref[...]).astype(BF16)
    y = _dot(h, w_ref[...])
    if scale != 1.0:
        y = y * scale
    o_ref[...] = y.astype(o_ref.dtype)


def _norm_matmul(x, g, w, *, tm, out_dtype, scale=1.0):
    m, d = x.shape
    n = w.shape[1]
    return pl.pallas_call(
        functools.partial(_norm_matmul_kernel, scale=scale),
        grid=(m // tm,),
        in_specs=[pl.BlockSpec((tm, d), lambda i: (i, 0)), _full((1, d)), _full(w.shape)],
        out_specs=pl.BlockSpec((tm, n), lambda i: (i, 0)),
        out_shape=jax.ShapeDtypeStruct((m, n), out_dtype),
        compiler_params=_cparams("parallel"),
        name="norm_matmul",
    )(x, g, w)


def _linres_kernel(*refs, n_in):
    a_refs, w_refs = refs[:n_in], refs[n_in:2 * n_in]
    x_ref, g_ref, o_ref = refs[2 * n_in:]
    y = _dot(a_refs[0][...], w_refs[0][...])
    for a_ref, w_ref in zip(a_refs[1:], w_refs[1:]):
        y = y + _dot(a_ref[...], w_ref[...])
    o_ref[...] = x_ref[...] + _rms(y, g_ref[...])


def _linear_residual(acts, ws, x, g, *, tm):
    m, d = x.shape
    n_in = len(acts)
    in_specs = [pl.BlockSpec((tm, a.shape[1]), lambda i: (i, 0)) for a in acts]
    in_specs += [_full(w.shape) for w in ws]
    in_specs += [pl.BlockSpec((tm, d), lambda i: (i, 0)), _full((1, d))]
    return pl.pallas_call(
        functools.partial(_linres_kernel, n_in=n_in),
        grid=(m // tm,),
        in_specs=in_specs,
        out_specs=pl.BlockSpec((tm, d), lambda i: (i, 0)),
        out_shape=jax.ShapeDtypeStruct((m, d), F32),
        compiler_params=_cparams("parallel"),
        name="linear_residual",
    )(*acts, *ws, x, g)


def _conv_module_kernel(u_ref, st_ref, w_ref, b_ref, lg_ref, lb_ref, o_ref, full_ref, *, tt, rows):
    t = pl.program_id(1)

    @pl.when(t == 0)
    def _():
        full_ref[0:CONV_HALO, :] = st_ref[0]

    @pl.when(t > 0)
    def _():
        full_ref[0:CONV_HALO, :] = full_ref[tt:tt + CONV_HALO, :]

    full_ref[CONV_HALO:CONV_HALO + tt, :] = u_ref[0]
    first = CONV_HALO - (CONV_WIDTH - 1)
    for r0 in range(0, tt, rows):
        acc = full_ref[r0 + first:r0 + first + rows, :] * w_ref[0:1, :]
        for j in range(1, CONV_WIDTH):
            acc = acc + full_ref[r0 + first + j:r0 + first + j + rows, :] * w_ref[j:j + 1, :]
        y = acc + b_ref[...]
        mu = jnp.mean(y, axis=-1, keepdims=True)
        yc = y - mu
        var = jnp.mean(yc * yc, axis=-1, keepdims=True)
        z = yc * lax.rsqrt(var + EPS) * lg_ref[...] + lb_ref[...]
        o_ref[0, r0:r0 + rows, :] = (z * _sigmoid(z)).astype(o_ref.dtype)


def _conv_module(u, state_pad, w_pad, b, ln_g, ln_b):
    bsz, seq, ch = u.shape
    tt = min(seq, 256)
    rows = min(tt, 32)
    return pl.pallas_call(
        functools.partial(_conv_module_kernel, tt=tt, rows=rows),
        grid=(bsz, seq // tt),
        in_specs=[pl.BlockSpec((1, tt, ch), lambda b_, t: (b_, t, 0)),
                  pl.BlockSpec((1, CONV_HALO, ch), lambda b_, t: (b_, 0, 0)),
                  _full(w_pad.shape), _full((1, ch)), _full((1, ch)), _full((1, ch))],
        out_specs=pl.BlockSpec((1, tt, ch), lambda b_, t: (b_, t, 0)),
        out_shape=jax.ShapeDtypeStruct((bsz, seq, ch), BF16),
        scratch_shapes=[pltpu.VMEM((CONV_HALO + tt, ch), F32)],
        compiler_params=_cparams("parallel", "arbitrary"),
        name="conv_module",
    )(u, state_pad, w_pad, b, ln_g, ln_b)


def _key_norm_kernel(k_ref, o_ref):
    sq = jnp.square(k_ref[0].astype(F32))
    row = lax.broadcasted_iota(jnp.int32, (LANES, LANES), 0)
    col = lax.broadcasted_iota(jnp.int32, (LANES, LANES), 1)
    per_head = _split_dot(sq, jnp.where(row // HEAD_DIM == col, 1.0, 0.0).astype(BF16))
    o_ref[0, 0] = jnp.broadcast_to(jnp.max(per_head, axis=0, keepdims=True), (SUBLANES, LANES))


def _key_norms(k):
    bsz, seq, width = k.shape
    pairs = width // (2 * HEAD_DIM)
    return pl.pallas_call(
        _key_norm_kernel,
        grid=(bsz, pairs),
        in_specs=[pl.BlockSpec((1, seq, LANES), lambda b, h: (b, 0, h))],
        out_specs=pl.BlockSpec((1, 1, SUBLANES, LANES), lambda b, h: (b, h, 0, 0)),
        out_shape=jax.ShapeDtypeStruct((bsz, pairs, SUBLANES, LANES), F32),
        compiler_params=_cparams("parallel", "parallel"),
        name="key_norms",
    )(k)


def _fox_attn_kernel(q_ref, k_ref, v_ref, cc_ref, cr_ref, kn_ref, o_ref, m_ref, l_ref, acc_ref, *, tq):
    qi = pl.program_id(2)
    m_ref[...] = jnp.full_like(m_ref, NEG)
    l_ref[...] = jnp.zeros_like(l_ref)
    acc_ref[...] = jnp.zeros_like(acc_ref)
    cc = cc_ref[0, 0]
    caps = []
    for hh in range(2):
        q32 = q_ref[0, :, hh * HEAD_DIM:(hh + 1) * HEAD_DIM].astype(F32)
        q2 = jnp.sum(q32 * q32, axis=-1, keepdims=True)
        caps.append(jnp.sqrt(q2 * kn_ref[0, 0, 0:1, hh:hh + 1]) * 1.01)

    def block(j, masked):
        ks = pl.multiple_of(j * tq, tq)
        for hh in range(2):
            sl = slice(hh * HEAD_DIM, (hh + 1) * HEAD_DIM)
            s = _dot_nt(q_ref[0, :, sl], k_ref[0, pl.ds(ks, tq), sl])
            s = s + (cc[:, hh:hh + 1] - cr_ref[0, 0, hh:hh + 1, pl.ds(ks, tq)])
            if masked:
                row = lax.broadcasted_iota(jnp.int32, (tq, tq), 0)
                col = lax.broadcasted_iota(jnp.int32, (tq, tq), 1)
                s = jnp.where(col <= row, s, NEG)
            m_prev = m_ref[hh]
            m_new = jnp.maximum(m_prev, jnp.max(s, axis=-1, keepdims=True))
            p = jnp.exp(s - m_new)
            alpha = jnp.exp(m_prev - m_new)
            l_ref[hh] = alpha * l_ref[hh] + jnp.sum(p, axis=-1, keepdims=True)
            acc_ref[hh] = alpha * acc_ref[hh] + _dot(p.astype(BF16), v_ref[0, pl.ds(ks, tq), sl])
            m_ref[hh] = m_new

    def headroom(j):
        ks = pl.multiple_of(j * tq, tq)
        tops = []
        for hh in range(2):
            c_min = jnp.min(cr_ref[0, 0, hh:hh + 1, pl.ds(ks, tq)], axis=-1, keepdims=True)
            tops.append(jnp.max(caps[hh] + cc[:, hh:hh + 1] - c_min - m_ref[hh]))
        return jnp.maximum(tops[0], tops[1])

    def more(carry):
        jj, top = carry
        return jnp.logical_and(jj < qi, top > SB_DEAD)

    def body(carry):
        jj, _ = carry
        j = qi - 1 - jj
        block(j, False)
        return jj + 1, headroom(jnp.maximum(j - 1, 0))

    block(qi, True)
    lax.while_loop(more, body, (jnp.int32(0), headroom(jnp.maximum(qi - 1, 0))))
    o_ref[0] = jnp.concatenate([acc_ref[hh] / l_ref[hh] for hh in range(2)], axis=-1).astype(o_ref.dtype)


def _fox_attention(q, k, v, c_col, c_row, *, tq):
    bsz, seq, width = q.shape
    pairs = width // (2 * HEAD_DIM)
    return pl.pallas_call(
        functools.partial(_fox_attn_kernel, tq=tq),
        grid=(bsz, pairs, seq // tq),
        in_specs=[pl.BlockSpec((1, tq, LANES), lambda b, h, i: (b, i, h)),
                  pl.BlockSpec((1, seq, LANES), lambda b, h, i: (b, 0, h)),
                  pl.BlockSpec((1, seq, LANES), lambda b, h, i: (b, 0, h)),
                  pl.BlockSpec((1, 1, tq, 2), lambda b, h, i: (b, h, i, 0)),
                  pl.BlockSpec((1, 1, 2, seq), lambda b, h, i: (b, h, 0, 0)),
                  pl.BlockSpec((1, 1, SUBLANES, LANES), lambda b, h, i: (b, h, 0, 0))],
        out_specs=pl.BlockSpec((1, tq, LANES), lambda b, h, i: (b, i, h)),
        out_shape=jax.ShapeDtypeStruct((bsz, seq, width), BF16),
        scratch_shapes=[pltpu.VMEM((2, tq, 1), F32), pltpu.VMEM((2, tq, 1), F32),
                        pltpu.VMEM((2, tq, HEAD_DIM), F32)],
        compiler_params=_cparams("parallel", "parallel", "arbitrary"),
        name="fox_attention",
    )(q, k, v, c_col, c_row, _key_norms(k))


def _later_key_matrix(n, dtype):
    row = lax.broadcasted_iota(jnp.int32, (n, n), 0)
    col = lax.broadcasted_iota(jnp.int32, (n, n), 1)
    return jnp.where(row > col, 1.0, 0.0).astype(dtype)


def _sb_block(q, k, v, later, r_prev, valid, transposed=False):
    z = _dot(q, k) if transposed else _dot_nt(q, k)
    log_remain = -_softplus(z)
    log_beta = z + log_remain
    if valid is not None:
        log_remain = jnp.where(valid, log_remain, 0.0)
    stick = _split_dot(log_remain, later) + r_prev
    w = jnp.exp(log_beta + stick)
    if valid is not None:
        w = jnp.where(valid, w, 0.0)
    r_new = stick[:, 0:1] + log_remain[:, 0:1]
    w = w.astype(BF16)
    return (_dot_nt(w, v) if transposed else _dot(w, v)), r_new


def _sb_attn_kernel(q_ref, k_ref, v_ref, o_ref, r_ref, acc_ref, *, tq):
    qi = pl.program_id(2)
    later = _later_key_matrix(tq, BF16)
    row = lax.broadcasted_iota(jnp.int32, (tq, tq), 0)
    col = lax.broadcasted_iota(jnp.int32, (tq, tq), 1)
    ks = pl.multiple_of(qi * tq, tq)
    for hh in range(2):
        sl = slice(hh * HEAD_DIM, (hh + 1) * HEAD_DIM)
        o, r = _sb_block(q_ref[0, :, sl], k_ref[0, pl.ds(ks, tq), sl], v_ref[0, pl.ds(ks, tq), sl],
                         later, 0.0, col < row)
        acc_ref[hh] = o
        r_ref[hh] = r

    def r_max():
        return jnp.max(jnp.maximum(r_ref[0], r_ref[1]))

    def more(carry):
        jj, r_top = carry
        return jnp.logical_and(jj < qi, r_top > SB_DEAD)

    def body(carry):
        jj, _ = carry
        ks_ = pl.multiple_of((qi - 1 - jj) * tq, tq)
        for hh in range(2):
            sl = slice(hh * HEAD_DIM, (hh + 1) * HEAD_DIM)
            o, r = _sb_block(q_ref[0, :, sl], k_ref[0, pl.ds(ks_, tq), sl], v_ref[0, pl.ds(ks_, tq), sl],
                             later, r_ref[hh], None)
            acc_ref[hh] = acc_ref[hh] + o
            r_ref[hh] = r
        return jj + 1, r_max()

    lax.while_loop(more, body, (jnp.int32(0), r_max()))
    o_ref[0] = jnp.concatenate([acc_ref[hh] for hh in range(2)], axis=-1).astype(o_ref.dtype)


def _sb_attention(q, k, v, *, tq):
    bsz, seq, width = q.shape
    pairs = width // (2 * HEAD_DIM)
    return pl.pallas_call(
        functools.partial(_sb_attn_kernel, tq=tq),
        grid=(bsz, pairs, seq // tq),
        in_specs=[pl.BlockSpec((1, tq, LANES), lambda b, h, i: (b, i, h)),
                  pl.BlockSpec((1, seq, LANES), lambda b, h, i: (b, 0, h)),
                  pl.BlockSpec((1, seq, LANES), lambda b, h, i: (b, 0, h))],
        out_specs=pl.BlockSpec((1, tq, LANES), lambda b, h, i: (b, i, h)),
        out_shape=jax.ShapeDtypeStruct((bsz, seq, width), BF16),
        scratch_shapes=[pltpu.VMEM((2, tq, 1), F32), pltpu.VMEM((2, tq, HEAD_DIM), F32)],
        compiler_params=_cparams("parallel", "parallel", "arbitrary"),
        name="sb_attention",
    )(q, k, v)


def _mem_attn_kernel(q_ref, k_ref, v_ref, o_ref):
    outs = []
    for h in range(MEM_HEADS):
        sl = slice(h * MEM_HEAD_DIM, (h + 1) * MEM_HEAD_DIM)
        s = _dot_nt(q_ref[0, :, sl].astype(BF16), k_ref[0, :, sl].astype(BF16))
        p = jnp.exp(s - jnp.max(s, axis=-1, keepdims=True))
        o = _dot(p.astype(BF16), v_ref[0, :, sl].astype(BF16))
        outs.append(o / jnp.sum(p, axis=-1, keepdims=True))
    o_ref[0] = jnp.concatenate(outs, axis=-1).astype(o_ref.dtype)


def _mem_attention(q, mk, mv, *, tm):
    bsz, seq, width = q.shape
    n_mem = mk.shape[1]
    return pl.pallas_call(
        _mem_attn_kernel,
        grid=(bsz, seq // tm),
        in_specs=[pl.BlockSpec((1, tm, width), lambda b, i: (b, i, 0)),
                  pl.BlockSpec((1, n_mem, width), lambda b, i: (b, 0, 0)),
                  pl.BlockSpec((1, n_mem, width), lambda b, i: (b, 0, 0))],
        out_specs=pl.BlockSpec((1, tm, width), lambda b, i: (b, i, 0)),
        out_shape=jax.ShapeDtypeStruct((bsz, seq, width), BF16),
        compiler_params=_cparams("parallel", "parallel"),
        name="mem_attention",
    )(q, mk, mv)


def _ffn_kernel(*refs, seq, tiles_per_seq):
    (x_ref, g4_ref, g5_ref, wg_ref, wv_ref, cwg_ref, cwv_ref, cbg_ref, cbv_ref, wd_ref) = refs[:10]
    if tiles_per_seq:
        sg_ref, sv_ref, o_ref, og_ref, ov_ref, h_ref, acc_ref, cg_ref, cv_ref = refs[10:]
    else:
        s1g_ref, s1v_ref, s2g_ref, s2v_ref, o_ref, og_ref, ov_ref, h_ref, acc_ref = refs[10:]
    i, j = pl.program_id(0), pl.program_id(1)
    tm = x_ref.shape[0]
    tf = wg_ref.shape[1]

    @pl.when(j == 0)
    def _():
        h_ref[...] = _rms(x_ref[...], g4_ref[...]).astype(BF16)
        acc_ref[...] = jnp.zeros_like(acc_ref)

    h = h_ref[...]
    rmod = lax.broadcasted_iota(jnp.int32, (tm, tf), 0) % seq

    def conv(u, w_ref, b_ref, prev1, prev2):
        s1 = jnp.where(rmod >= 1, pltpu.roll(u, 1, 0), prev1)
        s2 = jnp.where(rmod >= 2, pltpu.roll(u, 2, 0), prev2)
        return s2 * w_ref[0:1, :] + s1 * w_ref[1:2, :] + u * w_ref[2:3, :] + b_ref[...]

    ug = _dot(h, wg_ref[...])
    uv = _dot(h, wv_ref[...])
    if tiles_per_seq:
        @pl.when(i % tiles_per_seq == 0)
        def _():
            cg_ref[j] = sg_ref[0]
            cv_ref[j] = sv_ref[0]

        pg = cg_ref[j]
        pv = cv_ref[j]
        cg_ref[j] = ug[tm - 2:tm, :]
        cv_ref[j] = uv[tm - 2:tm, :]
        og_ref[0] = ug[tm - 2:tm, :]
        ov_ref[0] = uv[tm - 2:tm, :]
        cgate = conv(ug, cwg_ref, cbg_ref, pg[1:2, :], jnp.where(rmod == 0, pg[0:1, :], pg[1:2, :]))
        cval = conv(uv, cwv_ref, cbv_ref, pv[1:2, :], jnp.where(rmod == 0, pv[0:1, :], pv[1:2, :]))
    else:
        og_ref[...] = ug
        ov_ref[...] = uv
        cgate = conv(ug, cwg_ref, cbg_ref, s1g_ref[...], s2g_ref[...])
        cval = conv(uv, cwv_ref, cbv_ref, s1v_ref[...], s2v_ref[...])
    act = (cgate * _sigmoid(cgate) * cval).astype(BF16)
    acc_ref[...] += _dot(act, wd_ref[...])

    @pl.when(j == pl.num_programs(1) - 1)
    def _():
        o_ref[...] = x_ref[...] + _rms(acc_ref[...], g5_ref[...])


def _conv_ffn(x, g4, g5, w_up, conv_w, conv_b, w_down, state, *, bsz, seq, tm, tf):
    m, d = x.shape
    d_ff = w_down.shape[0]
    nf = d_ff // tf
    tiles_per_seq = seq // tm if seq % tm == 0 else 0
    kw = FFN_CONV_WIDTH
    in_specs = [pl.BlockSpec((tm, d), lambda i, j: (i, 0)), _full((1, d)), _full((1, d)),
                pl.BlockSpec((d, tf), lambda i, j: (0, j)),
                pl.BlockSpec((d, tf), lambda i, j: (0, nf + j)),
                pl.BlockSpec((kw, tf), lambda i, j: (0, j)),
                pl.BlockSpec((kw, tf), lambda i, j: (0, nf + j)),
                pl.BlockSpec((1, tf), lambda i, j: (0, j)),
                pl.BlockSpec((1, tf), lambda i, j: (0, nf + j)),
                pl.BlockSpec((tf, d), lambda i, j: (j, 0))]
    args = [x, g4, g5, w_up, w_up, conv_w, conv_w, conv_b, conv_b, w_down]
    scratch = [pltpu.VMEM((tm, d), BF16), pltpu.VMEM((tm, d), F32)]
    x_spec = pl.BlockSpec((tm, d), lambda i, j: (i, 0))
    if tiles_per_seq:
        in_specs += [pl.BlockSpec((1, kw - 1, tf), lambda i, j: (i // tiles_per_seq, 0, j)),
                     pl.BlockSpec((1, kw - 1, tf), lambda i, j: (i // tiles_per_seq, 0, nf + j))]
        args += [state, state]
        st_spec = pl.BlockSpec((1, kw - 1, tf), lambda i, j: (i, 0, j))
        out_specs = [x_spec, st_spec, st_spec]
        out_shape = [jax.ShapeDtypeStruct((m, d), F32),
                     jax.ShapeDtypeStruct((m // tm, kw - 1, d_ff), F32),
                     jax.ShapeDtypeStruct((m // tm, kw - 1, d_ff), F32)]
        scratch += [pltpu.VMEM((nf, kw - 1, tf), F32), pltpu.VMEM((nf, kw - 1, tf), F32)]
    else:
        assert tm % seq == 0 and seq >= kw - 1
        pad = seq - (kw - 1)
        s2 = jnp.pad(state, ((0, 0), (0, pad), (0, 0))).reshape(m, 2 * d_ff)
        s1 = jnp.pad(state[:, 1:], ((0, 0), (0, seq - 1), (0, 0))).reshape(m, 2 * d_ff)
        in_specs += [pl.BlockSpec((tm, tf), lambda i, j: (i, j)),
                     pl.BlockSpec((tm, tf), lambda i, j: (i, nf + j)),
                     pl.BlockSpec((tm, tf), lambda i, j: (i, j)),
                     pl.BlockSpec((tm, tf), lambda i, j: (i, nf + j))]
        args += [s1, s1, s2, s2]
        u_spec = pl.BlockSpec((tm, tf), lambda i, j: (i, j))
        out_specs = [x_spec, u_spec, u_spec]
        out_shape = [jax.ShapeDtypeStruct((m, d), F32), jax.ShapeDtypeStruct((m, d_ff), F32),
                     jax.ShapeDtypeStruct((m, d_ff), F32)]
    y, og, ov = pl.pallas_call(
        functools.partial(_ffn_kernel, seq=seq, tiles_per_seq=tiles_per_seq),
        grid=(m // tm, nf),
        in_specs=in_specs,
        out_specs=out_specs,
        out_shape=out_shape,
        scratch_shapes=scratch,
        compiler_params=_cparams("arbitrary", "arbitrary"),
        name="conv_ffn",
    )(*args)
    if tiles_per_seq:
        og = og[tiles_per_seq - 1::tiles_per_seq]
        ov = ov[tiles_per_seq - 1::tiles_per_seq]
    else:
        og = og.reshape(bsz, seq, d_ff)[:, seq - (kw - 1):]
        ov = ov.reshape(bsz, seq, d_ff)[:, seq - (kw - 1):]
    return y, jnp.concatenate([og, ov], axis=-1)


def _block_diag_queries(q, n_heads):
    t, width = q.shape
    tiled = jnp.concatenate([q.astype(F32)] * n_heads, axis=0)
    row = lax.broadcasted_iota(jnp.int32, (n_heads * t, width), 0)
    col = lax.broadcasted_iota(jnp.int32, (n_heads * t, width), 1)
    return jnp.where(row // t == col // HEAD_DIM, tiled, 0.0).astype(BF16)


def _diag_blocks(acc, t, n_heads):
    return jnp.concatenate(
        [acc[h * t:(h + 1) * t, h * HEAD_DIM:(h + 1) * HEAD_DIM] for h in range(n_heads)], axis=-1)


def _fox_decode_kernel(pt_ref, q_ref, kn_ref, vn_ref, lq_ref, *refs, pages_per_step, n_heads, t_new):
    npg = pages_per_step
    k_refs, v_refs, lf_refs = refs[:npg], refs[npg:2 * npg], refs[2 * npg:3 * npg]
    o_ref, qbd_ref, m_ref, l_ref, acc_ref, carry_ref, gq_ref = refs[3 * npg:]
    step = pl.program_id(1)
    rows = n_heads * t_new
    later = _later_key_matrix(LANES, F32)
    lane = lax.broadcasted_iota(jnp.int32, (rows, LANES), 1)
    qpos = lax.broadcasted_iota(jnp.int32, (rows, LANES), 0) % t_new

    def update(s, v):
        m_prev = m_ref[...]
        m_new = jnp.maximum(m_prev, jnp.max(s, axis=-1, keepdims=True))
        p = jnp.exp(s - m_new)
        alpha = jnp.exp(m_prev - m_new)
        l_ref[...] = alpha * l_ref[...] + jnp.sum(p, axis=-1, keepdims=True)
        acc_ref[...] = alpha * acc_ref[...] + _dot_nt(p.astype(BF16), v)
        m_ref[...] = m_new

    @pl.when(step == 0)
    def _():
        qbd_ref[...] = _block_diag_queries(q_ref[0], n_heads)
        m_ref[...] = jnp.full_like(m_ref, NEG)
        l_ref[...] = jnp.zeros_like(l_ref)
        acc_ref[...] = jnp.zeros_like(acc_ref)
        logf_rows = lq_ref[0]
        gap = _dot_exact(logf_rows, later)
        carry_ref[...] = gap[:, 0:1] + logf_rows[:, 0:1]
        gq_ref[...] = jnp.sum(jnp.where(lane == qpos, gap, 0.0), axis=-1, keepdims=True)
        s = (gap - gq_ref[...]) + _dot(qbd_ref[...], kn_ref[0])
        update(jnp.where(lane <= qpos, s, NEG), vn_ref[0])

    for p_ in range(npg):
        lf_t = lf_refs[p_][0]
        logf_rows = jnp.concatenate(
            [jnp.broadcast_to(lf_t[h:h + 1, :], (t_new, LANES)) for h in range(n_heads)], axis=0)
        gap = _dot_exact(logf_rows, later) + carry_ref[...]
        carry_ref[...] = gap[:, 0:1] + logf_rows[:, 0:1]
        s = (gap - gq_ref[...]) + _dot(qbd_ref[...], k_refs[p_][0].astype(BF16))
        update(s, v_refs[p_][0].astype(BF16))

    @pl.when(step == pl.num_programs(1) - 1)
    def _():
        o_ref[0] = _diag_blocks(acc_ref[...] / l_ref[...], t_new, n_heads).astype(o_ref.dtype)


def _sb_decode_kernel(pt_ref, q_ref, kn_ref, vn_ref, k_pool, v_pool, o_ref,
                      k_buf, v_buf, sems, qbd_ref, r_ref, acc_ref, *, n_pages, n_heads, t_new):
    b = pl.program_id(0)
    rows = n_heads * t_new
    later = _later_key_matrix(LANES, BF16)

    def page_copies(p, slot):
        page = pt_ref[b, n_pages - 1 - p]
        return (pltpu.make_async_copy(k_pool.at[page], k_buf.at[slot], sems.at[0, slot]),
                pltpu.make_async_copy(v_pool.at[page], v_buf.at[slot], sems.at[1, slot]))

    for cp in page_copies(0, 0):
        cp.start()

    qbd_ref[...] = _block_diag_queries(q_ref[0], n_heads)
    lane = lax.broadcasted_iota(jnp.int32, (rows, LANES), 1)
    qpos = lax.broadcasted_iota(jnp.int32, (rows, LANES), 0) % t_new
    o, r = _sb_block(qbd_ref[...], kn_ref[0], vn_ref[0], later, 0.0, lane < qpos, transposed=True)
    acc_ref[...] = o
    r_ref[...] = r

    def more(carry):
        p, r_top = carry
        return jnp.logical_and(p < n_pages, r_top > SB_DEAD)

    def body(carry):
        p, _ = carry
        slot = p % 2
        for cp in page_copies(p, slot):
            cp.wait()

        @pl.when(p + 1 < n_pages)
        def _():
            for cp in page_copies(p + 1, 1 - slot):
                cp.start()

        o, r = _sb_block(qbd_ref[...], k_buf[slot].astype(BF16), v_buf[slot].astype(BF16),
                         later, r_ref[...], None, transposed=True)
        acc_ref[...] += o
        r_ref[...] = r
        return p + 1, jnp.max(r)

    p_end, _ = lax.while_loop(more, body, (jnp.int32(0), jnp.max(r_ref[...])))

    @pl.when(p_end < n_pages)
    def _():
        for cp in page_copies(p_end, p_end % 2):
            cp.wait()

    o_ref[0] = _diag_blocks(acc_ref[...], t_new, n_heads).astype(o_ref.dtype)


def _sb_decode_attention(page_table, q, k_new, v_new, k_pool, v_pool):
    bsz, t_new, width = q.shape
    n_heads = width // HEAD_DIM
    n_pages = page_table.shape[1]
    page = k_pool.shape[2]
    assert page == LANES and t_new <= page
    rows = n_heads * t_new
    pad = ((0, 0), (0, 0), (0, page - t_new))
    kn, vn = jnp.pad(jnp.swapaxes(k_new, 1, 2), pad), jnp.pad(jnp.swapaxes(v_new, 1, 2), pad)
    per_seq = lambda shape: pl.BlockSpec(shape, lambda b, pt: (b, 0, 0))
    return pl.pallas_call(
        functools.partial(_sb_decode_kernel, n_pages=n_pages, n_heads=n_heads, t_new=t_new),
        grid_spec=pltpu.PrefetchScalarGridSpec(
            num_scalar_prefetch=1,
            grid=(bsz,),
            in_specs=[per_seq((1, t_new, width)), per_seq((1, width, page)), per_seq((1, width, page)),
                      pl.BlockSpec(memory_space=pl.ANY), pl.BlockSpec(memory_space=pl.ANY)],
            out_specs=per_seq((1, t_new, width)),
            scratch_shapes=[pltpu.VMEM((2, width, page), F32), pltpu.VMEM((2, width, page), F32),
                            pltpu.SemaphoreType.DMA((2, 2)),
                            pltpu.VMEM((rows, width), BF16), pltpu.VMEM((rows, 1), F32),
                            pltpu.VMEM((rows, width), F32)]),
        out_shape=jax.ShapeDtypeStruct((bsz, t_new, width), F32),
        compiler_params=_cparams("arbitrary"),
        name="sb_decode_attention",
    )(page_table, q, kn, vn, k_pool, v_pool)


def _fox_decode_attention(page_table, q, k_new, v_new, k_pool, v_pool, logf_new, logf_pool, *, pages_per_step):
    bsz, t_new, width = q.shape
    n_heads = width // HEAD_DIM
    n_pages = page_table.shape[1]
    page = k_pool.shape[2]
    assert page == LANES and t_new <= page and n_pages % pages_per_step == 0
    n_steps = n_pages // pages_per_step
    rows = n_heads * t_new
    pad = ((0, 0), (0, 0), (0, page - t_new))
    kn, vn = jnp.pad(jnp.swapaxes(k_new, 1, 2), pad), jnp.pad(jnp.swapaxes(v_new, 1, 2), pad)

    def page_spec(shape, p_):
        def index(b, s, pt):
            return (pt[b, n_pages - 1 - (s * pages_per_step + p_)], 0, 0)
        return pl.BlockSpec(shape, index)

    per_seq = lambda shape: pl.BlockSpec(shape, lambda b, s, pt: (b, 0, 0))
    in_specs = [per_seq((1, t_new, width)), per_seq((1, width, page)), per_seq((1, width, page))]
    args = [q, kn, vn]
    kv_specs = [page_spec((1, width, page), p_) for p_ in range(pages_per_step)]
    lq = jnp.repeat(jnp.swapaxes(logf_new, 1, 2), t_new, axis=1)
    lq = jnp.pad(lq, ((0, 0), (0, 0), (0, page - t_new)))
    lf_t = jnp.swapaxes(logf_pool, 1, 2)
    in_specs += [per_seq((1, rows, page))] + kv_specs + kv_specs
    in_specs += [page_spec((1, n_heads, page), p_) for p_ in range(pages_per_step)]
    args += [lq] + [k_pool] * pages_per_step + [v_pool] * pages_per_step + [lf_t] * pages_per_step
    scratch = [pltpu.VMEM((rows, width), BF16), pltpu.VMEM((rows, 1), F32), pltpu.VMEM((rows, 1), F32),
               pltpu.VMEM((rows, width), F32), pltpu.VMEM((rows, 1), F32), pltpu.VMEM((rows, 1), F32)]
    return pl.pallas_call(
        functools.partial(_fox_decode_kernel, pages_per_step=pages_per_step, n_heads=n_heads, t_new=t_new),
        grid_spec=pltpu.PrefetchScalarGridSpec(
            num_scalar_prefetch=1,
            grid=(bsz, n_steps),
            in_specs=in_specs,
            out_specs=pl.BlockSpec((1, t_new, width), lambda b, s, pt: (b, 0, 0)),
            scratch_shapes=scratch),
        out_shape=jax.ShapeDtypeStruct((bsz, t_new, width), F32),
        compiler_params=_cparams("parallel", "arbitrary"),
        name="fox_decode_attention",
    )(page_table, *args)


def _prep_weights(p):
    w = {}
    n_even = p["even_w_in"].shape[0]
    ch = p["conv_w"].shape[2]
    even_in = p["even_w_in"].shape[2]
    n_fox = even_in - 5 * ch
    w["even_w_in"] = jnp.pad(p["even_w_in"], ((0, 0), (0, 0), (0, LANES - n_fox))).astype(BF16)
    w["even_b_f"] = jnp.pad(p["even_b_f"], ((0, 0), (0, LANES - n_fox))).reshape(n_even, 1, LANES)
    w["conv_w"] = jnp.pad(p["conv_w"], ((0, 0), (0, CONV_HALO - CONV_WIDTH), (0, 0)))
    for name in ("even_w_out", "sb_w_in", "sb_w_out", "mem_wq", "mem_wo", "ffn_w_up", "ffn_w_down"):
        w[name] = p[name].astype(BF16)
    w["mem_wkv"] = jnp.concatenate([p["mem_wk"], p["mem_wv"]], axis=-1).astype(BF16)
    return w


def _pages_transposed(pool):
    n, page, heads, dim = pool.shape
    return jnp.transpose(pool, (0, 2, 3, 1)).reshape(n, heads * dim, page)


def _run_trunk(x, mem_k, mem_v, conv_state, ffn_state, fox_cache, sb_cache, page_table, p, w):
    bsz, seq, d = x.shape
    m = bsz * seq
    depth = p["norm_g"].shape[0]
    ch = p["conv_w"].shape[2]
    n_fox = p["even_b_f"].shape[1]
    prompt = fox_cache is None
    tm = min(m, 256)
    tq = min(seq, 256)
    x = x.reshape(m, d)
    fox_k, fox_v, fox_logf, conv_new, sb_k, sb_v, ffn_new = [], [], [], [], [], [], []
    for layer in range(depth):
        g = p["norm_g"][layer].reshape(6, 1, d)
        i = layer // 2
        if layer % 2 == 0:
            u, q, k, v, kb, vb, logf, cum = _even_proj(
                x, g[0], w["even_w_in"][i], w["even_b_f"][i], ch=ch, nh=n_fox, seq=seq, tm=tm)
            u3 = u.reshape(bsz, seq, ch)
            state = conv_state[i]
            halo = CONV_WIDTH - 1
            state_pad = jnp.pad(state, ((0, 0), (CONV_HALO - halo, 0), (0, 0)))
            a_out = _conv_module(u3, state_pad, w["conv_w"][i], p["conv_b"][i].reshape(1, ch),
                                 p["conv_ln_g"][i].reshape(1, ch), p["conv_ln_b"][i].reshape(1, ch))
            conv_new.append(jnp.concatenate([state, u3], axis=1)[:, seq:])
            q3, kb3, vb3 = (a.reshape(bsz, seq, ch) for a in (q, kb, vb))
            if prompt:
                c4 = cum.reshape(bsz, seq, n_fox // 2, 2)
                o = _fox_attention(q3, kb3, vb3, jnp.transpose(c4, (0, 2, 1, 3)),
                                   jnp.transpose(c4, (0, 2, 3, 1)), tq=tq)
            else:
                pool = lambda c: _pages_transposed(c[i])
                o = _fox_decode_attention(page_table, q3.astype(F32), kb3, vb3, pool(fox_cache[0]),
                                          pool(fox_cache[1]), logf.reshape(bsz, seq, n_fox), fox_cache[2][i],
                                          pages_per_step=4)
            x = _linear_residual([a_out.reshape(m, ch), o.reshape(m, ch).astype(BF16)],
                                 [w["even_w_out"][i][:ch], w["even_w_out"][i][ch:]], x, g[1], tm=tm)
            fox_k.append(k.reshape(bsz, seq, n_fox, HEAD_DIM))
            fox_v.append(v.reshape(bsz, seq, n_fox, HEAD_DIM))
            fox_logf.append(logf.reshape(bsz, seq, n_fox))
        else:
            q, k, v, kb, vb = _odd_proj(x, g[0], w["sb_w_in"][i], tm=tm)
            width = q.shape[1]
            q3, kb3, vb3 = (a.reshape(bsz, seq, width) for a in (q, kb, vb))
            if prompt:
                o = _sb_attention(q3, kb3, vb3, tq=tq)
            else:
                pool = lambda c: _pages_transposed(c[i])
                o = _sb_decode_attention(page_table, q3.astype(F32), kb3, vb3, pool(sb_cache[0]), pool(sb_cache[1]))
            x = _linear_residual([o.reshape(m, width).astype(BF16)], [w["sb_w_out"][i]], x, g[1], tm=tm)
            sb_k.append(k.reshape(bsz, seq, width // HEAD_DIM, HEAD_DIM))
            sb_v.append(v.reshape(bsz, seq, width // HEAD_DIM, HEAD_DIM))
        qm = _norm_matmul(x, g[2], w["mem_wq"][layer], tm=tm, out_dtype=BF16 if seq % 16 == 0 else F32,
                          scale=MEM_HEAD_DIM ** -0.5)
        mw = qm.shape[1]
        om = _mem_attention(qm.reshape(bsz, seq, mw), mem_k[layer], mem_v[layer], tm=min(seq, 512))
        x = _linear_residual([om.reshape(m, mw)], [w["mem_wo"][layer]], x, g[3], tm=tm)
        d_ff2 = p["ffn_w_up"].shape[2]
        x, buf = _conv_ffn(x, g[4], g[5], w["ffn_w_up"][layer], p["ffn_conv_w"][layer],
                           p["ffn_conv_b"][layer].reshape(1, d_ff2), w["ffn_w_down"][layer],
                           ffn_state[layer], bsz=bsz, seq=seq, tm=min(m, 512), tf=256)
        ffn_new.append(buf)
    return (x.reshape(bsz, seq, d), jnp.stack(fox_k), jnp.stack(fox_v), jnp.stack(fox_logf),
            jnp.stack(conv_new), jnp.stack(sb_k), jnp.stack(sb_v), jnp.stack(ffn_new))


def kernel(x_prompt, x_sample, cache_fox_k, cache_fox_v, cache_fox_logf, state_conv, cache_sb_k, cache_sb_v, cache_mem_k, cache_mem_v, state_ffn_conv, page_table, mem_prompt, norm_g, even_w_in, even_b_f, conv_w, conv_b, conv_ln_g, conv_ln_b, even_w_out, sb_w_in, sb_w_out, mem_norm_g, mem_wq, mem_wk, mem_wv, mem_wo, ffn_w_up, ffn_conv_w, ffn_conv_b, ffn_w_down):
    p = dict(norm_g=norm_g, even_w_in=even_w_in, even_b_f=even_b_f, conv_w=conv_w, conv_b=conv_b,
             conv_ln_g=conv_ln_g, conv_ln_b=conv_ln_b, even_w_out=even_w_out, sb_w_in=sb_w_in,
             sb_w_out=sb_w_out, mem_wq=mem_wq, mem_wk=mem_wk, mem_wv=mem_wv, mem_wo=mem_wo,
             ffn_w_up=ffn_w_up, ffn_conv_w=ffn_conv_w, ffn_conv_b=ffn_conv_b, ffn_w_down=ffn_w_down)
    w = _prep_weights(p)
    depth, d = norm_g.shape[0], norm_g.shape[2]
    bsz, n_mem = mem_prompt.shape[0], mem_prompt.shape[1]
    mem_width = mem_wq.shape[2]
    n_even = even_w_in.shape[0]
    ch = conv_w.shape[2]

    mem_flat = mem_prompt.reshape(bsz * n_mem, d)
    mem_kv = [_norm_matmul(mem_flat, mem_norm_g[l].reshape(1, d), w["mem_wkv"][l], tm=256, out_dtype=F32)
              for l in range(depth)]
    mem_k_prompt = jnp.stack([kv[:, :mem_width].reshape(bsz, n_mem, mem_width) for kv in mem_kv])
    mem_v_prompt = jnp.stack([kv[:, mem_width:].reshape(bsz, n_mem, mem_width) for kv in mem_kv])
    conv_zero = jnp.zeros((n_even, bsz, CONV_WIDTH - 1, ch), F32)
    ffn_zero = jnp.zeros((depth, bsz, FFN_CONV_WIDTH - 1, ffn_w_up.shape[2]), F32)
    (y_prompt, fox_k_prompt, fox_v_prompt, fox_logf_prompt, conv_state_prompt,
     sb_k_prompt, sb_v_prompt, ffn_state_prompt) = _run_trunk(
        x_prompt, mem_k_prompt, mem_v_prompt, conv_zero, ffn_zero, None, None, None, p, w)

    dbsz = x_sample.shape[0]
    cmk = cache_mem_k.reshape(depth, dbsz, n_mem, mem_width)
    cmv = cache_mem_v.reshape(depth, dbsz, n_mem, mem_width)
    (y_sample, fox_k_sample, fox_v_sample, fox_logf_sample, conv_state_sample,
     sb_k_sample, sb_v_sample, ffn_state_sample) = _run_trunk(
        x_sample, cmk, cmv, state_conv, state_ffn_conv,
        (cache_fox_k, cache_fox_v, cache_fox_logf), (cache_sb_k, cache_sb_v), page_table, p, w)

    mem_shape = (depth, bsz, n_mem, MEM_HEADS, MEM_HEAD_DIM)
    return (y_prompt, y_sample, fox_k_prompt, fox_v_prompt, fox_logf_prompt, conv_state_prompt,
            sb_k_prompt, sb_v_prompt, mem_k_prompt.reshape(mem_shape), mem_v_prompt.reshape(mem_shape),
            ffn_state_prompt, fox_k_sample, fox_v_sample, fox_logf_sample, conv_state_sample,
            sb_k_sample, sb_v_sample, ffn_state_sample)
```

```python
import functools

import jax
import jax.numpy as jnp
from jax import lax
from jax.experimental import pallas as pl
from jax.experimental.pallas import tpu as pltpu

F32 = jnp.float32
BF16 = jnp.bfloat16

EPS = 1e-6
NEG = -1e30
SB_DEAD = -105.0
HEAD_DIM = 64
MEM_HEADS = 4
MEM_HEAD_DIM = 128
CONV_WIDTH = 31
FFN_CONV_WIDTH = 3
LANES = 128
SUBLANES = 8
CONV_HALO = 32
VMEM_LIMIT = 48 * 1024 * 1024


def _cparams(*sem):
    return pltpu.CompilerParams(dimension_semantics=sem, vmem_limit_bytes=VMEM_LIMIT)


def _dot(a, b):
    return jnp.dot(a, b, preferred_element_type=F32)


def _dot_nt(a, b):
    return lax.dot_general(a, b, (((1,), (1,)), ((), ())), preferred_element_type=F32)


def _dot_exact(a, b):
    return jnp.dot(a, b, preferred_element_type=F32, precision=lax.Precision.HIGHEST)


def _rms(x, g):
    return x * lax.rsqrt(jnp.mean(x * x, axis=-1, keepdims=True) + EPS) * g


def _softplus(z):
    return jnp.maximum(z, 0.0) + jnp.log(1.0 + jnp.exp(-jnp.abs(z)))


def _sigmoid(z):
    return 1.0 / (1.0 + jnp.exp(-z))


def _split_dot(x, u):
    hi = x.astype(BF16)
    lo = (x - hi.astype(F32)).astype(BF16)
    return _dot(hi, u) + _dot(lo, u)


def _full(shape):
    nd = len(shape)
    return pl.BlockSpec(shape, lambda *_: (0,) * nd)


def _even_proj_kernel(x_ref, g_ref, w_ref, bf_ref, u_ref, q_ref, k_ref, v_ref, kb_ref, vb_ref,
                      lf_ref, c_ref, carry_ref, *, ch, tiles_per_seq):
    tm = x_ref.shape[0]
    h = _rms(x_ref[...], g_ref[...]).astype(BF16)
    ag = _dot(h, w_ref[:, 0:2 * ch])
    u_ref[...] = ag[:, :ch] * _sigmoid(ag[:, ch:])
    q_ref[...] = (_dot(h, w_ref[:, 2 * ch:3 * ch]) * HEAD_DIM ** -0.5).astype(BF16)
    k = _dot(h, w_ref[:, 3 * ch:4 * ch])
    k_ref[...] = k
    kb_ref[...] = k.astype(BF16)
    v = _dot(h, w_ref[:, 4 * ch:5 * ch])
    v_ref[...] = v
    vb_ref[...] = v.astype(BF16)
    f = _dot(h, w_ref[:, 5 * ch:5 * ch + LANES]) + bf_ref[...]
    lf = -_softplus(-f)
    nh = lf_ref.shape[1]
    lf_ref[...] = lf[:, :nh]
    if tiles_per_seq:
        @pl.when(pl.program_id(0) % tiles_per_seq == 0)
        def _():
            carry_ref[...] = jnp.zeros_like(carry_ref)
        row = lax.broadcasted_iota(jnp.int32, (tm, tm), 0)
        col = lax.broadcasted_iota(jnp.int32, (tm, tm), 1)
        tri = jnp.where(row >= col, 1.0, 0.0).astype(F32)
        c = _dot_exact(tri, lf) + carry_ref[...]
        carry_ref[...] = c[tm - 1:tm, :]
        c_ref[...] = c[:, :nh]
    else:
        c_ref[...] = lf[:, :nh]


def _even_proj(x, g, w_pad, bf_pad, *, ch, nh, seq, tm):
    m, d = x.shape
    tiles_per_seq = seq // tm if seq % tm == 0 else 0
    row = lambda n: pl.BlockSpec((tm, n), lambda i: (i, 0))
    outs = [jax.ShapeDtypeStruct((m, ch), F32), jax.ShapeDtypeStruct((m, ch), BF16),
            jax.ShapeDtypeStruct((m, ch), F32), jax.ShapeDtypeStruct((m, ch), F32),
            jax.ShapeDtypeStruct((m, ch), BF16), jax.ShapeDtypeStruct((m, ch), BF16),
            jax.ShapeDtypeStruct((m, nh), F32), jax.ShapeDtypeStruct((m, nh), F32)]
    return pl.pallas_call(
        functools.partial(_even_proj_kernel, ch=ch, tiles_per_seq=tiles_per_seq),
        grid=(m // tm,),
        in_specs=[row(d), _full((1, d)), _full(w_pad.shape), _full((1, LANES))],
        out_specs=[row(ch)] * 6 + [row(nh)] * 2,
        out_shape=outs,
        scratch_shapes=[pltpu.VMEM((1, LANES), F32)],
        compiler_params=_cparams("arbitrary"),
        name="even_proj",
    )(x, g, w_pad, bf_pad)


def _odd_proj_kernel(x_ref, g_ref, w_ref, q_ref, k_ref, v_ref, kb_ref, vb_ref, *, width):
    h = _rms(x_ref[...], g_ref[...]).astype(BF16)
    q_ref[...] = (_dot(h, w_ref[:, 0:width]) * HEAD_DIM ** -0.5).astype(BF16)
    k = _dot(h, w_ref[:, width:2 * width])
    k_ref[...] = k
    kb_ref[...] = k.astype(BF16)
    v = _dot(h, w_ref[:, 2 * width:3 * width])
    v_ref[...] = v
    vb_ref[...] = v.astype(BF16)


def _odd_proj(x, g, w, *, tm):
    m, d = x.shape
    width = w.shape[1] // 3
    row = lambda n: pl.BlockSpec((tm, n), lambda i: (i, 0))
    outs = [jax.ShapeDtypeStruct((m, width), BF16), jax.ShapeDtypeStruct((m, width), F32),
            jax.ShapeDtypeStruct((m, width), F32), jax.ShapeDtypeStruct((m, width), BF16),
            jax.ShapeDtypeStruct((m, width), BF16)]
    return pl.pallas_call(
        functools.partial(_odd_proj_kernel, width=width),
        grid=(m // tm,),
        in_specs=[row(d), _full((1, d)), _full(w.shape)],
        out_specs=[row(width)] * 5,
        out_shape=outs,
        compiler_params=_cparams("parallel"),
        name="odd_proj",
    )(x, g, w)


def _norm_matmul_kernel(x_ref, g_ref, w_ref, o_ref, *, scale):
    h = _rms(x_ref[...], g_ref[...]).astype(BF16)
    y = _dot(h, w_ref[...])
    if scale != 1.0:
        y = y * scale
    o_ref[...] = y.astype(o_ref.dtype)


def _norm_matmul(x, g, w, *, tm, out_dtype, scale=1.0):
    m, d = x.shape
    n = w.shape[1]
    return pl.pallas_call(
        functools.partial(_norm_matmul_kernel, scale=scale),
        grid=(m // tm,),
        in_specs=[pl.BlockSpec((tm, d), lambda i: (i, 0)), _full((1, d)), _full(w.shape)],
        out_specs=pl.BlockSpec((tm, n), lambda i: (i, 0)),
        out_shape=jax.ShapeDtypeStruct((m, n), out_dtype),
        compiler_params=_cparams("parallel"),
        name="norm_matmul",
    )(x, g, w)


def _linres_kernel(*refs, n_in):
    a_refs, w_refs = refs[:n_in], refs[n_in:2 * n_in]
    x_ref, g_ref, o_ref = refs[2 * n_in:]
    y = _dot(a_refs[0][...], w_refs[0][...])
    for a_ref, w_ref in zip(a_refs[1:], w_refs[1:]):
        y = y + _dot(a_ref[...], w_ref[...])
    o_ref[...] = x_ref[...] + _rms(y, g_ref[...])


def _linear_residual(acts, ws, x, g, *, tm):
    m, d = x.shape
    n_in = len(acts)
    in_specs = [pl.BlockSpec((tm, a.shape[1]), lambda i: (i, 0)) for a in acts]
    in_specs += [_full(w.shape) for w in ws]
    in_specs += [pl.BlockSpec((tm, d), lambda i: (i, 0)), _full((1, d))]
    return pl.pallas_call(
        functools.partial(_linres_kernel, n_in=n_in),
        grid=(m // tm,),
        in_specs=in_specs,
        out_specs=pl.BlockSpec((tm, d), lambda i: (i, 0)),
        out_shape=jax.ShapeDtypeStruct((m, d), F32),
        compiler_params=_cparams("parallel"),
        name="linear_residual",
    )(*acts, *ws, x, g)


def _conv_module_kernel(u_ref, st_ref, w_ref, b_ref, lg_ref, lb_ref, o_ref, full_ref, *, tt, rows):
    t = pl.program_id(1)

    @pl.when(t == 0)
    def _():
        full_ref[0:CONV_HALO, :] = st_ref[0]

    @pl.when(t > 0)
    def _():
        full_ref[0:CONV_HALO, :] = full_ref[tt:tt + CONV_HALO, :]

    full_ref[CONV_HALO:CONV_HALO + tt, :] = u_ref[0]
    first = CONV_HALO - (CONV_WIDTH - 1)
    for r0 in range(0, tt, rows):
        acc = full_ref[r0 + first:r0 + first + rows, :] * w_ref[0:1, :]
        for j in range(1, CONV_WIDTH):
            acc = acc + full_ref[r0 + first + j:r0 + first + j + rows, :] * w_ref[j:j + 1, :]
        y = acc + b_ref[...]
        mu = jnp.mean(y, axis=-1, keepdims=True)
        yc = y - mu
        var = jnp.mean(yc * yc, axis=-1, keepdims=True)
        z = yc * lax.rsqrt(var + EPS) * lg_ref[...] + lb_ref[...]
        o_ref[0, r0:r0 + rows, :] = (z * _sigmoid(z)).astype(o_ref.dtype)


def _conv_module(u, state_pad, w_pad, b, ln_g, ln_b):
    bsz, seq, ch = u.shape
    tt = min(seq, 256)
    rows = min(tt, 32)
    return pl.pallas_call(
        functools.partial(_conv_module_kernel, tt=tt, rows=rows),
        grid=(bsz, seq // tt),
        in_specs=[pl.BlockSpec((1, tt, ch), lambda b_, t: (b_, t, 0)),
                  pl.BlockSpec((1, CONV_HALO, ch), lambda b_, t: (b_, 0, 0)),
                  _full(w_pad.shape), _full((1, ch)), _full((1, ch)), _full((1, ch))],
        out_specs=pl.BlockSpec((1, tt, ch), lambda b_, t: (b_, t, 0)),
        out_shape=jax.ShapeDtypeStruct((bsz, seq, ch), BF16),
        scratch_shapes=[pltpu.VMEM((CONV_HALO + tt, ch), F32)],
        compiler_params=_cparams("parallel", "arbitrary"),
        name="conv_module",
    )(u, state_pad, w_pad, b, ln_g, ln_b)


def _key_norm_kernel(k_ref, o_ref):
    sq = jnp.square(k_ref[0].astype(F32))
    row = lax.broadcasted_iota(jnp.int32, (LANES, LANES), 0)
    col = lax.broadcasted_iota(jnp.int32, (LANES, LANES), 1)
    per_head = _split_dot(sq, jnp.where(row // HEAD_DIM == col, 1.0, 0.0).astype(BF16))
    o_ref[0, 0] = jnp.broadcast_to(jnp.max(per_head, axis=0, keepdims=True), (SUBLANES, LANES))


def _key_norms(k):
    bsz, seq, width = k.shape
    pairs = width // (2 * HEAD_DIM)
    return pl.pallas_call(
        _key_norm_kernel,
        grid=(bsz, pairs),
        in_specs=[pl.BlockSpec((1, seq, LANES), lambda b, h: (b, 0, h))],
        out_specs=pl.BlockSpec((1, 1, SUBLANES, LANES), lambda b, h: (b, h, 0, 0)),
        out_shape=jax.ShapeDtypeStruct((bsz, pairs, SUBLANES, LANES), F32),
        compiler_params=_cparams("parallel", "parallel"),
        name="key_norms",
    )(k)


def _fox_attn_kernel(q_ref, k_ref, v_ref, cc_ref, cr_ref, kn_ref, o_ref, m_ref, l_ref, acc_ref, *, tq):
    qi = pl.program_id(2)
    m_ref[...] = jnp.full_like(m_ref, NEG)
    l_ref[...] = jnp.zeros_like(l_ref)
    acc_ref[...] = jnp.zeros_like(acc_ref)
    cc = cc_ref[0, 0]
    caps = []
    for hh in range(2):
        q32 = q_ref[0, :, hh * HEAD_DIM:(hh + 1) * HEAD_DIM].astype(F32)
        q2 = jnp.sum(q32 * q32, axis=-1, keepdims=True)
        caps.append(jnp.sqrt(q2 * kn_ref[0, 0, 0:1, hh:hh + 1]) * 1.01)

    def block(j, masked):
        ks = pl.multiple_of(j * tq, tq)
        for hh in range(2):
            sl = slice(hh * HEAD_DIM, (hh + 1) * HEAD_DIM)
            s = _dot_nt(q_ref[0, :, sl], k_ref[0, pl.ds(ks, tq), sl])
            s = s + (cc[:, hh:hh + 1] - cr_ref[0, 0, hh:hh + 1, pl.ds(ks, tq)])
            if masked:
                row = lax.broadcasted_iota(jnp.int32, (tq, tq), 0)
                col = lax.broadcasted_iota(jnp.int32, (tq, tq), 1)
                s = jnp.where(col <= row, s, NEG)
            m_prev = m_ref[hh]
            m_new = jnp.maximum(m_prev, jnp.max(s, axis=-1, keepdims=True))
            p = jnp.exp(s - m_new)
            alpha = jnp.exp(m_prev - m_new)
            l_ref[hh] = alpha * l_ref[hh] + jnp.sum(p, axis=-1, keepdims=True)
            acc_ref[hh] = alpha * acc_ref[hh] + _dot(p.astype(BF16), v_ref[0, pl.ds(ks, tq), sl])
            m_ref[hh] = m_new

    def headroom(j):
        ks = pl.multiple_of(j * tq, tq)
        tops = []
        for hh in range(2):
            c_min = jnp.min(cr_ref[0, 0, hh:hh + 1, pl.ds(ks, tq)], axis=-1, keepdims=True)
            tops.append(jnp.max(caps[hh] + cc[:, hh:hh + 1] - c_min - m_ref[hh]))
        return jnp.maximum(tops[0], tops[1])

    def more(carry):
        jj, top = carry
        return jnp.logical_and(jj < qi, top > SB_DEAD)

    def body(carry):
        jj, _ = carry
        j = qi - 1 - jj
        block(j, False)
        return jj + 1, headroom(jnp.maximum(j - 1, 0))

    block(qi, True)
    lax.while_loop(more, body, (jnp.int32(0), headroom(jnp.maximum(qi - 1, 0))))
    o_ref[0] = jnp.concatenate([acc_ref[hh] / l_ref[hh] for hh in range(2)], axis=-1).astype(o_ref.dtype)


def _fox_attention(q, k, v, c_col, c_row, *, tq):
    bsz, seq, width = q.shape
    pairs = width // (2 * HEAD_DIM)
    return pl.pallas_call(
        functools.partial(_fox_attn_kernel, tq=tq),
        grid=(bsz, pairs, seq // tq),
        in_specs=[pl.BlockSpec((1, tq, LANES), lambda b, h, i: (b, i, h)),
                  pl.BlockSpec((1, seq, LANES), lambda b, h, i: (b, 0, h)),
                  pl.BlockSpec((1, seq, LANES), lambda b, h, i: (b, 0, h)),
                  pl.BlockSpec((1, 1, tq, 2), lambda b, h, i: (b, h, i, 0)),
                  pl.BlockSpec((1, 1, 2, seq), lambda b, h, i: (b, h, 0, 0)),
                  pl.BlockSpec((1, 1, SUBLANES, LANES), lambda b, h, i: (b, h, 0, 0))],
        out_specs=pl.BlockSpec((1, tq, LANES), lambda b, h, i: (b, i, h)),
        out_shape=jax.ShapeDtypeStruct((bsz, seq, width), BF16),
        scratch_shapes=[pltpu.VMEM((2, tq, 1), F32), pltpu.VMEM((2, tq, 1), F32),
                        pltpu.VMEM((2, tq, HEAD_DIM), F32)],
        compiler_params=_cparams("parallel", "parallel", "arbitrary"),
        name="fox_attention",
    )(q, k, v, c_col, c_row, _key_norms(k))


def _later_key_matrix(n, dtype):
    row = lax.broadcasted_iota(jnp.int32, (n, n), 0)
    col = lax.broadcasted_iota(jnp.int32, (n, n), 1)
    return jnp.where(row > col, 1.0, 0.0).astype(dtype)


def _sb_block(q, k, v, later, r_prev, valid, transposed=False):
    z = _dot(q, k) if transposed else _dot_nt(q, k)
    log_remain = -_softplus(z)
    log_beta = z + log_remain
    if valid is not None:
        log_remain = jnp.where(valid, log_remain, 0.0)
    stick = _split_dot(log_remain, later) + r_prev
    w = jnp.exp(log_beta + stick)
    if valid is not None:
        w = jnp.where(valid, w, 0.0)
    r_new = stick[:, 0:1] + log_remain[:, 0:1]
    w = w.astype(BF16)
    return (_dot_nt(w, v) if transposed else _dot(w, v)), r_new


def _sb_attn_kernel(q_ref, k_ref, v_ref, o_ref, r_ref, acc_ref, *, tq):
    qi = pl.program_id(2)
    later = _later_key_matrix(tq, BF16)
    row = lax.broadcasted_iota(jnp.int32, (tq, tq), 0)
    col = lax.broadcasted_iota(jnp.int32, (tq, tq), 1)
    ks = pl.multiple_of(qi * tq, tq)
    for hh in range(2):
        sl = slice(hh * HEAD_DIM, (hh + 1) * HEAD_DIM)
        o, r = _sb_block(q_ref[0, :, sl], k_ref[0, pl.ds(ks, tq), sl], v_ref[0, pl.ds(ks, tq), sl],
                         later, 0.0, col < row)
        acc_ref[hh] = o
        r_ref[hh] = r

    def r_max():
        return jnp.max(jnp.maximum(r_ref[0], r_ref[1]))

    def more(carry):
        jj, r_top = carry
        return jnp.logical_and(jj < qi, r_top > SB_DEAD)

    def body(carry):
        jj, _ = carry
        ks_ = pl.multiple_of((qi - 1 - jj) * tq, tq)
        for hh in range(2):
            sl = slice(hh * HEAD_DIM, (hh + 1) * HEAD_DIM)
            o, r = _sb_block(q_ref[0, :, sl], k_ref[0, pl.ds(ks_, tq), sl], v_ref[0, pl.ds(ks_, tq), sl],
                             later, r_ref[hh], None)
            acc_ref[hh] = acc_ref[hh] + o
            r_ref[hh] = r
        return jj + 1, r_max()

    lax.while_loop(more, body, (jnp.int32(0), r_max()))
    o_ref[0] = jnp.concatenate([acc_ref[hh] for hh in range(2)], axis=-1).astype(o_ref.dtype)


def _sb_attention(q, k, v, *, tq):
    bsz, seq, width = q.shape
    pairs = width // (2 * HEAD_DIM)
    return pl.pallas_call(
        functools.partial(_sb_attn_kernel, tq=tq),
        grid=(bsz, pairs, seq // tq),
        in_specs=[pl.BlockSpec((1, tq, LANES), lambda b, h, i: (b, i, h)),
                  pl.BlockSpec((1, seq, LANES), lambda b, h, i: (b, 0, h)),
                  pl.BlockSpec((1, seq, LANES), lambda b, h, i: (b, 0, h))],
        out_specs=pl.BlockSpec((1, tq, LANES), lambda b, h, i: (b, i, h)),
        out_shape=jax.ShapeDtypeStruct((bsz, seq, width), BF16),
        scratch_shapes=[pltpu.VMEM((2, tq, 1), F32), pltpu.VMEM((2, tq, HEAD_DIM), F32)],
        compiler_params=_cparams("parallel", "parallel", "arbitrary"),
        name="sb_attention",
    )(q, k, v)


def _mem_attn_kernel(q_ref, k_ref, v_ref, o_ref):
    outs = []
    for h in range(MEM_HEADS):
        sl = slice(h * MEM_HEAD_DIM, (h + 1) * MEM_HEAD_DIM)
        s = _dot_nt(q_ref[0, :, sl].astype(BF16), k_ref[0, :, sl].astype(BF16))
        p = jnp.exp(s - jnp.max(s, axis=-1, keepdims=True))
        o = _dot(p.astype(BF16), v_ref[0, :, sl].astype(BF16))
        outs.append(o / jnp.sum(p, axis=-1, keepdims=True))
    o_ref[0] = jnp.concatenate(outs, axis=-1).astype(o_ref.dtype)


def _mem_attention(q, mk, mv, *, tm):
    bsz, seq, width = q.shape
    n_mem = mk.shape[1]
    return pl.pallas_call(
        _mem_attn_kernel,
        grid=(bsz, seq // tm),
        in_specs=[pl.BlockSpec((1, tm, width), lambda b, i: (b, i, 0)),
                  pl.BlockSpec((1, n_mem, width), lambda b, i: (b, 0, 0)),
                  pl.BlockSpec((1, n_mem, width), lambda b, i: (b, 0, 0))],
        out_specs=pl.BlockSpec((1, tm, width), lambda b, i: (b, i, 0)),
        out_shape=jax.ShapeDtypeStruct((bsz, seq, width), BF16),
        compiler_params=_cparams("parallel", "parallel"),
        name="mem_attention",
    )(q, mk, mv)


def _ffn_kernel(*refs, seq, tiles_per_seq):
    (x_ref, g4_ref, g5_ref, wg_ref, wv_ref, cwg_ref, cwv_ref, cbg_ref, cbv_ref, wd_ref) = refs[:10]
    if tiles_per_seq:
        sg_ref, sv_ref, o_ref, og_ref, ov_ref, h_ref, acc_ref, cg_ref, cv_ref = refs[10:]
    else:
        s1g_ref, s1v_ref, s2g_ref, s2v_ref, o_ref, og_ref, ov_ref, h_ref, acc_ref = refs[10:]
    i, j = pl.program_id(0), pl.program_id(1)
    tm = x_ref.shape[0]
    tf = wg_ref.shape[1]

    @pl.when(j == 0)
    def _():
        h_ref[...] = _rms(x_ref[...], g4_ref[...]).astype(BF16)
        acc_ref[...] = jnp.zeros_like(acc_ref)

    h = h_ref[...]

    def conv(u, w_ref, b_ref, prev1, prev2):
        rmod = lax.broadcasted_iota(jnp.int32, (tm, tf), 0) % seq
        s1 = jnp.where(rmod >= 1, pltpu.roll(u, 1, 0), prev1)
        s2 = jnp.where(rmod >= 2, pltpu.roll(u, 2, 0), prev2)
        return s2 * w_ref[0:1, :] + s1 * w_ref[1:2, :] + u * w_ref[2:3, :] + b_ref[...]

    ug = _dot(h, wg_ref[...])
    uv = _dot(h, wv_ref[...])
    if tiles_per_seq:
        @pl.when(i % tiles_per_seq == 0)
        def _():
            cg_ref[j] = sg_ref[0]
            cv_ref[j] = sv_ref[0]

        pg = cg_ref[j]
        pv = cv_ref[j]
        cg_ref[j] = ug[tm - 2:tm, :]
        cv_ref[j] = uv[tm - 2:tm, :]
        og_ref[0] = ug[tm - 2:tm, :]
        ov_ref[0] = uv[tm - 2:tm, :]
        rmod = lax.broadcasted_iota(jnp.int32, (tm, tf), 0)
        cgate = conv(ug, cwg_ref, cbg_ref, pg[1:2, :], jnp.where(rmod == 0, pg[0:1, :], pg[1:2, :]))
        cval = conv(uv, cwv_ref, cbv_ref, pv[1:2, :], jnp.where(rmod == 0, pv[0:1, :], pv[1:2, :]))
    else:
        og_ref[...] = ug
        ov_ref[...] = uv
        cgate = conv(ug, cwg_ref, cbg_ref, s1g_ref[...], s2g_ref[...])
        cval = conv(uv, cwv_ref, cbv_ref, s1v_ref[...], s2v_ref[...])
    act = (cgate * _sigmoid(cgate) * cval).astype(BF16)
    acc_ref[...] += _dot(act, wd_ref[...])

    @pl.when(j == pl.num_programs(1) - 1)
    def _():
        o_ref[...] = x_ref[...] + _rms(acc_ref[...], g5_ref[...])


def _conv_ffn(x, g4, g5, w_up, conv_w, conv_b, w_down, state, *, bsz, seq, tm, tf):
    m, d = x.shape
    d_ff = w_down.shape[0]
    nf = d_ff // tf
    tiles_per_seq = seq // tm if seq % tm == 0 else 0
    kw = FFN_CONV_WIDTH
    in_specs = [pl.BlockSpec((tm, d), lambda i, j: (i, 0)), _full((1, d)), _full((1, d)),
                pl.BlockSpec((d, tf), lambda i, j: (0, j)),
                pl.BlockSpec((d, tf), lambda i, j: (0, nf + j)),
                pl.BlockSpec((kw, tf), lambda i, j: (0, j)),
                pl.BlockSpec((kw, tf), lambda i, j: (0, nf + j)),
                pl.BlockSpec((1, tf), lambda i, j: (0, j)),
                pl.BlockSpec((1, tf), lambda i, j: (0, nf + j)),
                pl.BlockSpec((tf, d), lambda i, j: (j, 0))]
    args = [x, g4, g5, w_up, w_up, conv_w, conv_w, conv_b, conv_b, w_down]
    scratch = [pltpu.VMEM((tm, d), BF16), pltpu.VMEM((tm, d), F32)]
    x_spec = pl.BlockSpec((tm, d), lambda i, j: (i, 0))
    if tiles_per_seq:
        in_specs += [pl.BlockSpec((1, kw - 1, tf), lambda i, j: (i // tiles_per_seq, 0, j)),
                     pl.BlockSpec((1, kw - 1, tf), lambda i, j: (i // tiles_per_seq, 0, nf + j))]
        args += [state, state]
        st_spec = pl.BlockSpec((1, kw - 1, tf), lambda i, j: (i, 0, j))
        out_specs = [x_spec, st_spec, st_spec]
        out_shape = [jax.ShapeDtypeStruct((m, d), F32),
                     jax.ShapeDtypeStruct((m // tm, kw - 1, d_ff), F32),
                     jax.ShapeDtypeStruct((m // tm, kw - 1, d_ff), F32)]
        scratch += [pltpu.VMEM((nf, kw - 1, tf), F32), pltpu.VMEM((nf, kw - 1, tf), F32)]
    else:
        assert tm % seq == 0 and seq >= kw - 1
        pad = seq - (kw - 1)
        s2 = jnp.pad(state, ((0, 0), (0, pad), (0, 0))).reshape(m, 2 * d_ff)
        s1 = jnp.pad(state[:, 1:], ((0, 0), (0, seq - 1), (0, 0))).reshape(m, 2 * d_ff)
        in_specs += [pl.BlockSpec((tm, tf), lambda i, j: (i, j)),
                     pl.BlockSpec((tm, tf), lambda i, j: (i, nf + j)),
                     pl.BlockSpec((tm, tf), lambda i, j: (i, j)),
                     pl.BlockSpec((tm, tf), lambda i, j: (i, nf + j))]
        args += [s1, s1, s2, s2]
        u_spec = pl.BlockSpec((tm, tf), lambda i, j: (i, j))
        out_specs = [x_spec, u_spec, u_spec]
        out_shape = [jax.ShapeDtypeStruct((m, d), F32), jax.ShapeDtypeStruct((m, d_ff), F32),
                     jax.ShapeDtypeStruct((m, d_ff), F32)]
    y, og, ov = pl.pallas_call(
        functools.partial(_ffn_kernel, seq=seq, tiles_per_seq=tiles_per_seq),
        grid=(m // tm, nf),
        in_specs=in_specs,
        out_specs=out_specs,
        out_shape=out_shape,
        scratch_shapes=scratch,
        compiler_params=_cparams("arbitrary", "arbitrary"),
        name="conv_ffn",
    )(*args)
    if tiles_per_seq:
        og = og[tiles_per_seq - 1::tiles_per_seq]
        ov = ov[tiles_per_seq - 1::tiles_per_seq]
    else:
        og = og.reshape(bsz, seq, d_ff)[:, seq - (kw - 1):]
        ov = ov.reshape(bsz, seq, d_ff)[:, seq - (kw - 1):]
    return y, jnp.concatenate([og, ov], axis=-1)


def _block_diag_queries(q, n_heads):
    t, width = q.shape
    tiled = jnp.concatenate([q.astype(F32)] * n_heads, axis=0)
    row = lax.broadcasted_iota(jnp.int32, (n_heads * t, width), 0)
    col = lax.broadcasted_iota(jnp.int32, (n_heads * t, width), 1)
    return jnp.where(row // t == col // HEAD_DIM, tiled, 0.0).astype(BF16)


def _diag_blocks(acc, t, n_heads):
    return jnp.concatenate(
        [acc[h * t:(h + 1) * t, h * HEAD_DIM:(h + 1) * HEAD_DIM] for h in range(n_heads)], axis=-1)


def _pool_key_norm_kernel(k_ref, o_ref, *, n_heads):
    @pl.when(pl.program_id(0) == 0)
    def _():
        o_ref[...] = jnp.zeros_like(o_ref)

    pages, width, page = k_ref.shape
    sq = jnp.square(k_ref[...]).reshape(pages, n_heads, width // n_heads, page)
    o_ref[...] = jnp.maximum(o_ref[...], jnp.max(jnp.sum(sq, axis=2), axis=0))


def _pool_key_norms(k_pool, n_heads):
    n_pool, width, page = k_pool.shape
    pages = max(p_ for p_ in (8, 4, 2, 1) if n_pool % p_ == 0)
    return pl.pallas_call(
        functools.partial(_pool_key_norm_kernel, n_heads=n_heads),
        grid=(n_pool // pages,),
        in_specs=[pl.BlockSpec((pages, width, page), lambda i: (i, 0, 0))],
        out_specs=pl.BlockSpec((n_heads, page), lambda i: (0, 0)),
        out_shape=jax.ShapeDtypeStruct((n_heads, page), F32),
        compiler_params=_cparams("arbitrary"),
        name="pool_key_norms",
    )(k_pool)


def _fox_decode_kernel(pt_ref, q_ref, kn_ref, vn_ref, lq_ref, kmax_ref, k_pool, v_pool, lf_pool, o_ref,
                       k_buf, v_buf, lf_buf, sems, qbd_ref, m_ref, l_ref, acc_ref, carry_ref, gq_ref,
                       *, n_pages, n_heads, t_new):
    b = pl.program_id(0)
    rows = n_heads * t_new
    later = _later_key_matrix(LANES, F32)
    lane = lax.broadcasted_iota(jnp.int32, (rows, LANES), 1)
    qpos = lax.broadcasted_iota(jnp.int32, (rows, LANES), 0) % t_new

    def page_copies(p, slot):
        page = pt_ref[b, n_pages - 1 - p]
        return (pltpu.make_async_copy(k_pool.at[page], k_buf.at[slot], sems.at[0, slot]),
                pltpu.make_async_copy(v_pool.at[page], v_buf.at[slot], sems.at[1, slot]),
                pltpu.make_async_copy(lf_pool.at[page], lf_buf.at[slot], sems.at[2, slot]))

    for cp in page_copies(0, 0):
        cp.start()

    def rows_of_heads(per_head):
        n = per_head.shape[1]
        return jnp.concatenate(
            [jnp.broadcast_to(per_head[h:h + 1, :], (t_new, n)) for h in range(n_heads)], axis=0)

    def update(s, v):
        m_prev = m_ref[...]
        m_new = jnp.maximum(m_prev, jnp.max(s, axis=-1, keepdims=True))
        p = jnp.exp(s - m_new)
        alpha = jnp.exp(m_prev - m_new)
        l_ref[...] = alpha * l_ref[...] + jnp.sum(p, axis=-1, keepdims=True)
        acc_ref[...] = alpha * acc_ref[...] + _dot_nt(p.astype(BF16), v)
        m_ref[...] = m_new

    qbd_ref[...] = _block_diag_queries(q_ref[0], n_heads)
    m_ref[...] = jnp.full_like(m_ref, NEG)
    l_ref[...] = jnp.zeros_like(l_ref)
    acc_ref[...] = jnp.zeros_like(acc_ref)
    logf_rows = lq_ref[0]
    gap = _dot_exact(logf_rows, later)
    carry_ref[...] = gap[:, 0:1] + logf_rows[:, 0:1]
    gq_ref[...] = jnp.sum(jnp.where(lane == qpos, gap, 0.0), axis=-1, keepdims=True)
    s = (gap - gq_ref[...]) + _dot(qbd_ref[...], kn_ref[0])
    update(jnp.where(lane <= qpos, s, NEG), vn_ref[0])

    q32 = qbd_ref[...].astype(F32)
    k2_max = rows_of_heads(jnp.max(kmax_ref[...], axis=-1, keepdims=True))
    cap = jnp.sqrt(jnp.sum(q32 * q32, axis=-1, keepdims=True) * k2_max) * 1.01

    def headroom():
        return jnp.max(cap + carry_ref[...] - gq_ref[...] - m_ref[...])

    def more(carry):
        p, top = carry
        return jnp.logical_and(p < n_pages, top > SB_DEAD)

    def body(carry):
        p, _ = carry
        slot = p % 2
        for cp in page_copies(p, slot):
            cp.wait()

        @pl.when(p + 1 < n_pages)
        def _():
            for cp in page_copies(p + 1, 1 - slot):
                cp.start()

        logf_rows = rows_of_heads(lf_buf[slot])
        gap = _dot_exact(logf_rows, later) + carry_ref[...]
        carry_ref[...] = gap[:, 0:1] + logf_rows[:, 0:1]
        s = (gap - gq_ref[...]) + _dot(qbd_ref[...], k_buf[slot].astype(BF16))
        update(s, v_buf[slot].astype(BF16))
        return p + 1, headroom()

    p_end, _ = lax.while_loop(more, body, (jnp.int32(0), headroom()))

    @pl.when(p_end < n_pages)
    def _():
        for cp in page_copies(p_end, p_end % 2):
            cp.wait()

    o_ref[0] = _diag_blocks(acc_ref[...] / l_ref[...], t_new, n_heads).astype(o_ref.dtype)


def _fox_decode_attention(page_table, q, k_new, v_new, k_pool, v_pool, logf_new, logf_pool):
    bsz, t_new, width = q.shape
    n_heads = width // HEAD_DIM
    n_pages = page_table.shape[1]
    page = k_pool.shape[2]
    assert page == LANES and t_new <= page
    rows = n_heads * t_new
    pad = ((0, 0), (0, 0), (0, page - t_new))
    kn, vn = jnp.pad(jnp.swapaxes(k_new, 1, 2), pad), jnp.pad(jnp.swapaxes(v_new, 1, 2), pad)
    lq = jnp.pad(jnp.repeat(jnp.swapaxes(logf_new, 1, 2), t_new, axis=1), pad)
    lf_t = jnp.swapaxes(logf_pool, 1, 2)
    per_seq = lambda shape: pl.BlockSpec(shape, lambda b, pt: (b, 0, 0))
    anywhere = pl.BlockSpec(memory_space=pl.ANY)
    return pl.pallas_call(
        functools.partial(_fox_decode_kernel, n_pages=n_pages, n_heads=n_heads, t_new=t_new),
        grid_spec=pltpu.PrefetchScalarGridSpec(
            num_scalar_prefetch=1,
            grid=(bsz,),
            in_specs=[per_seq((1, t_new, width)), per_seq((1, width, page)), per_seq((1, width, page)),
                      per_seq((1, rows, page)), pl.BlockSpec((n_heads, page), lambda b, pt: (0, 0)),
                      anywhere, anywhere, anywhere],
            out_specs=per_seq((1, t_new, width)),
            scratch_shapes=[pltpu.VMEM((2, width, page), F32), pltpu.VMEM((2, width, page), F32),
                            pltpu.VMEM((2, n_heads, page), F32), pltpu.SemaphoreType.DMA((3, 2)),
                            pltpu.VMEM((rows, width), BF16), pltpu.VMEM((rows, 1), F32),
                            pltpu.VMEM((rows, 1), F32), pltpu.VMEM((rows, width), F32),
                            pltpu.VMEM((rows, 1), F32), pltpu.VMEM((rows, 1), F32)]),
        out_shape=jax.ShapeDtypeStruct((bsz, t_new, width), F32),
        compiler_params=_cparams("arbitrary"),
        name="fox_decode_attention",
    )(page_table, q, kn, vn, lq, _pool_key_norms(k_pool, n_heads), k_pool, v_pool, lf_t)


def _sb_decode_kernel(pt_ref, q_ref, kn_ref, vn_ref, k_pool, v_pool, o_ref,
                      k_buf, v_buf, sems, qbd_ref, r_ref, acc_ref, *, n_pages, n_heads, t_new):
    b = pl.program_id(0)
    rows = n_heads * t_new
    later = _later_key_matrix(LANES, BF16)

    def page_copies(p, slot):
        page = pt_ref[b, n_pages - 1 - p]
        return (pltpu.make_async_copy(k_pool.at[page], k_buf.at[slot], sems.at[0, slot]),
                pltpu.make_async_copy(v_pool.at[page], v_buf.at[slot], sems.at[1, slot]))

    for cp in page_copies(0, 0):
        cp.start()

    qbd_ref[...] = _block_diag_queries(q_ref[0], n_heads)
    lane = lax.broadcasted_iota(jnp.int32, (rows, LANES), 1)
    qpos = lax.broadcasted_iota(jnp.int32, (rows, LANES), 0) % t_new
    o, r = _sb_block(qbd_ref[...], kn_ref[0], vn_ref[0], later, 0.0, lane < qpos, transposed=True)
    acc_ref[...] = o
    r_ref[...] = r

    def more(carry):
        p, r_top = carry
        return jnp.logical_and(p < n_pages, r_top > SB_DEAD)

    def body(carry):
        p, _ = carry
        slot = p % 2
        for cp in page_copies(p, slot):
            cp.wait()

        @pl.when(p + 1 < n_pages)
        def _():
            for cp in page_copies(p + 1, 1 - slot):
                cp.start()

        o, r = _sb_block(qbd_ref[...], k_buf[slot].astype(BF16), v_buf[slot].astype(BF16),
                         later, r_ref[...], None, transposed=True)
        acc_ref[...] += o
        r_ref[...] = r
        return p + 1, jnp.max(r)

    p_end, _ = lax.while_loop(more, body, (jnp.int32(0), jnp.max(r_ref[...])))

    @pl.when(p_end < n_pages)
    def _():
        for cp in page_copies(p_end, p_end % 2):
            cp.wait()

    o_ref[0] = _diag_blocks(acc_ref[...], t_new, n_heads).astype(o_ref.dtype)


def _sb_decode_attention(page_table, q, k_new, v_new, k_pool, v_pool):
    bsz, t_new, width = q.shape
    n_heads = width // HEAD_DIM
    n_pages = page_table.shape[1]
    page = k_pool.shape[2]
    assert page == LANES and t_new <= page
    rows = n_heads * t_new
    pad = ((0, 0), (0, 0), (0, page - t_new))
    kn, vn = jnp.pad(jnp.swapaxes(k_new, 1, 2), pad), jnp.pad(jnp.swapaxes(v_new, 1, 2), pad)
    per_seq = lambda shape: pl.BlockSpec(shape, lambda b, pt: (b, 0, 0))
    return pl.pallas_call(
        functools.partial(_sb_decode_kernel, n_pages=n_pages, n_heads=n_heads, t_new=t_new),
        grid_spec=pltpu.PrefetchScalarGridSpec(
            num_scalar_prefetch=1,
            grid=(bsz,),
            in_specs=[per_seq((1, t_new, width)), per_seq((1, width, page)), per_seq((1, width, page)),
                      pl.BlockSpec(memory_space=pl.ANY), pl.BlockSpec(memory_space=pl.ANY)],
            out_specs=per_seq((1, t_new, width)),
            scratch_shapes=[pltpu.VMEM((2, width, page), F32), pltpu.VMEM((2, width, page), F32),
                            pltpu.SemaphoreType.DMA((2, 2)),
                            pltpu.VMEM((rows, width), BF16), pltpu.VMEM((rows, 1), F32),
                            pltpu.VMEM((rows, width), F32)]),
        out_shape=jax.ShapeDtypeStruct((bsz, t_new, width), F32),
        compiler_params=_cparams("arbitrary"),
        name="sb_decode_attention",
    )(page_table, q, kn, vn, k_pool, v_pool)


def _prep_weights(p):
    w = {}
    n_even = p["even_w_in"].shape[0]
    ch = p["conv_w"].shape[2]
    even_in = p["even_w_in"].shape[2]
    n_fox = even_in - 5 * ch
    w["even_w_in"] = jnp.pad(p["even_w_in"], ((0, 0), (0, 0), (0, LANES - n_fox))).astype(BF16)
    w["even_b_f"] = jnp.pad(p["even_b_f"], ((0, 0), (0, LANES - n_fox))).reshape(n_even, 1, LANES)
    w["conv_w"] = jnp.pad(p["conv_w"], ((0, 0), (0, CONV_HALO - CONV_WIDTH), (0, 0)))
    for name in ("even_w_out", "sb_w_in", "sb_w_out", "mem_wq", "mem_wo", "ffn_w_up", "ffn_w_down"):
        w[name] = p[name].astype(BF16)
    w["mem_wkv"] = jnp.concatenate([p["mem_wk"], p["mem_wv"]], axis=-1).astype(BF16)
    return w


def _pages_transposed(pool):
    n, page, heads, dim = pool.shape
    return jnp.transpose(pool, (0, 2, 3, 1)).reshape(n, heads * dim, page)


def _run_trunk(x, mem_k, mem_v, conv_state, ffn_state, fox_cache, sb_cache, page_table, p, w):
    bsz, seq, d = x.shape
    m = bsz * seq
    depth = p["norm_g"].shape[0]
    ch = p["conv_w"].shape[2]
    n_fox = p["even_b_f"].shape[1]
    prompt = fox_cache is None
    tm = min(m, 256)
    tq = min(seq, 256)
    x = x.reshape(m, d)
    fox_k, fox_v, fox_logf, conv_new, sb_k, sb_v, ffn_new = [], [], [], [], [], [], []
    for layer in range(depth):
        g = p["norm_g"][layer].reshape(6, 1, d)
        i = layer // 2
        if layer % 2 == 0:
            u, q, k, v, kb, vb, logf, cum = _even_proj(
                x, g[0], w["even_w_in"][i], w["even_b_f"][i], ch=ch, nh=n_fox, seq=seq, tm=tm)
            u3 = u.reshape(bsz, seq, ch)
            state = conv_state[i]
            halo = CONV_WIDTH - 1
            state_pad = jnp.pad(state, ((0, 0), (CONV_HALO - halo, 0), (0, 0)))
            a_out = _conv_module(u3, state_pad, w["conv_w"][i], p["conv_b"][i].reshape(1, ch),
                                 p["conv_ln_g"][i].reshape(1, ch), p["conv_ln_b"][i].reshape(1, ch))
            conv_new.append(jnp.concatenate([state, u3], axis=1)[:, seq:])
            q3, kb3, vb3 = (a.reshape(bsz, seq, ch) for a in (q, kb, vb))
            if prompt:
                c4 = cum.reshape(bsz, seq, n_fox // 2, 2)
                o = _fox_attention(q3, kb3, vb3, jnp.transpose(c4, (0, 2, 1, 3)),
                                   jnp.transpose(c4, (0, 2, 3, 1)), tq=tq)
            else:
                pool = lambda c: _pages_transposed(c[i])
                o = _fox_decode_attention(page_table, q3.astype(F32), kb3, vb3, pool(fox_cache[0]),
                                          pool(fox_cache[1]), logf.reshape(bsz, seq, n_fox), fox_cache[2][i])
            x = _linear_residual([a_out.reshape(m, ch), o.reshape(m, ch).astype(BF16)],
                                 [w["even_w_out"][i][:ch], w["even_w_out"][i][ch:]], x, g[1], tm=tm)
            fox_k.append(k.reshape(bsz, seq, n_fox, HEAD_DIM))
            fox_v.append(v.reshape(bsz, seq, n_fox, HEAD_DIM))
            fox_logf.append(logf.reshape(bsz, seq, n_fox))
        else:
            q, k, v, kb, vb = _odd_proj(x, g[0], w["sb_w_in"][i], tm=tm)
            width = q.shape[1]
            q3, kb3, vb3 = (a.reshape(bsz, seq, width) for a in (q, kb, vb))
            if prompt:
                o = _sb_attention(q3, kb3, vb3, tq=tq)
            else:
                pool = lambda c: _pages_transposed(c[i])
                o = _sb_decode_attention(page_table, q3.astype(F32), kb3, vb3, pool(sb_cache[0]), pool(sb_cache[1]))
            x = _linear_residual([o.reshape(m, width).astype(BF16)], [w["sb_w_out"][i]], x, g[1], tm=tm)
            sb_k.append(k.reshape(bsz, seq, width // HEAD_DIM, HEAD_DIM))
            sb_v.append(v.reshape(bsz, seq, width // HEAD_DIM, HEAD_DIM))
        qm = _norm_matmul(x, g[2], w["mem_wq"][layer], tm=tm, out_dtype=BF16 if seq % 16 == 0 else F32,
                          scale=MEM_HEAD_DIM ** -0.5)
        mw = qm.shape[1]
        om = _mem_attention(qm.reshape(bsz, seq, mw), mem_k[layer], mem_v[layer], tm=min(seq, 512))
        x = _linear_residual([om.reshape(m, mw)], [w["mem_wo"][layer]], x, g[3], tm=tm)
        d_ff2 = p["ffn_w_up"].shape[2]
        x, buf = _conv_ffn(x, g[4], g[5], w["ffn_w_up"][layer], p["ffn_conv_w"][layer],
                           p["ffn_conv_b"][layer].reshape(1, d_ff2), w["ffn_w_down"][layer],
                           ffn_state[layer], bsz=bsz, seq=seq, tm=min(m, 512), tf=256)
        ffn_new.append(buf)
    return (x.reshape(bsz, seq, d), jnp.stack(fox_k), jnp.stack(fox_v), jnp.stack(fox_logf),
            jnp.stack(conv_new), jnp.stack(sb_k), jnp.stack(sb_v), jnp.stack(ffn_new))


def kernel(x_prompt, x_sample, cache_fox_k, cache_fox_v, cache_fox_logf, state_conv, cache_sb_k, cache_sb_v, cache_mem_k, cache_mem_v, state_ffn_conv, page_table, mem_prompt, norm_g, even_w_in, even_b_f, conv_w, conv_b, conv_ln_g, conv_ln_b, even_w_out, sb_w_in, sb_w_out, mem_norm_g, mem_wq, mem_wk, mem_wv, mem_wo, ffn_w_up, ffn_conv_w, ffn_conv_b, ffn_w_down):
    p = dict(norm_g=norm_g, even_w_in=even_w_in, even_b_f=even_b_f, conv_w=conv_w, conv_b=conv_b,
             conv_ln_g=conv_ln_g, conv_ln_b=conv_ln_b, even_w_out=even_w_out, sb_w_in=sb_w_in,
             sb_w_out=sb_w_out, mem_wq=mem_wq, mem_wk=mem_wk, mem_wv=mem_wv, mem_wo=mem_wo,
             ffn_w_up=ffn_w_up, ffn_conv_w=ffn_conv_w, ffn_conv_b=ffn_conv_b, ffn_w_down=ffn_w_down)
    w = _prep_weights(p)
    depth, d = norm_g.shape[0], norm_g.shape[2]
    bsz, n_mem = mem_prompt.shape[0], mem_prompt.shape[1]
    mem_width = mem_wq.shape[2]
    n_even = even_w_in.shape[0]
    ch = conv_w.shape[2]

    mem_flat = mem_prompt.reshape(bsz * n_mem, d)
    mem_kv = [_norm_matmul(mem_flat, mem_norm_g[l].reshape(1, d), w["mem_wkv"][l], tm=256, out_dtype=F32)
              for l in range(depth)]
    mem_k_prompt = jnp.stack([kv[:, :mem_width].reshape(bsz, n_mem, mem_width) for kv in mem_kv])
    mem_v_prompt = jnp.stack([kv[:, mem_width:].reshape(bsz, n_mem, mem_width) for kv in mem_kv])
    conv_zero = jnp.zeros((n_even, bsz, CONV_WIDTH - 1, ch), F32)
    ffn_zero = jnp.zeros((depth, bsz, FFN_CONV_WIDTH - 1, ffn_w_up.shape[2]), F32)
    (y_prompt, fox_k_prompt, fox_v_prompt, fox_logf_prompt, conv_state_prompt,
     sb_k_prompt, sb_v_prompt, ffn_state_prompt) = _run_trunk(
        x_prompt, mem_k_prompt, mem_v_prompt, conv_zero, ffn_zero, None, None, None, p, w)

    dbsz = x_sample.shape[0]
    cmk = cache_mem_k.reshape(depth, dbsz, n_mem, mem_width)
    cmv = cache_mem_v.reshape(depth, dbsz, n_mem, mem_width)
    (y_sample, fox_k_sample, fox_v_sample, fox_logf_sample, conv_state_sample,
     sb_k_sample, sb_v_sample, ffn_state_sample) = _run_trunk(
        x_sample, cmk, cmv, state_conv, state_ffn_conv,
        (cache_fox_k, cache_fox_v, cache_fox_logf), (cache_sb_k, cache_sb_v), page_table, p, w)

    mem_shape = (depth, bsz, n_mem, MEM_HEADS, MEM_HEAD_DIM)
    return (y_prompt, y_sample, fox_k_prompt, fox_v_prompt, fox_logf_prompt, conv_state_prompt,
            sb_k_prompt, sb_v_prompt, mem_k_prompt.reshape(mem_shape), mem_v_prompt.reshape(mem_shape),
            ffn_state_prompt, fox_k_sample, fox_v_sample, fox_logf_sample, conv_state_sample,
            sb_k_sample, sb_v_sample, ffn_state_sample)
```

```python
import functools

import jax
import jax.numpy as jnp
from jax import lax
from jax.experimental import pallas as pl
from jax.experimental.pallas import tpu as pltpu

F32 = jnp.float32
BF16 = jnp.bfloat16

EPS = 1e-6
NEG = -1e30
SB_DEAD = -105.0
HEAD_DIM = 64
MEM_HEADS = 4
MEM_HEAD_DIM = 128
CONV_WIDTH = 31
FFN_CONV_WIDTH = 3
LANES = 128
SUBLANES = 8
CONV_HALO = 32
VMEM_LIMIT = 48 * 1024 * 1024


def _cparams(*sem):
    return pltpu.CompilerParams(dimension_semantics=sem, vmem_limit_bytes=VMEM_LIMIT)


def _dot(a, b):
    return jnp.dot(a, b, preferred_element_type=F32)


def _dot_nt(a, b):
    return lax.dot_general(a, b, (((1,), (1,)), ((), ())), preferred_element_type=F32)


def _dot_exact(a, b):
    return jnp.dot(a, b, preferred_element_type=F32, precision=lax.Precision.HIGHEST)


def _rms(x, g):
    return x * lax.rsqrt(jnp.mean(x * x, axis=-1, keepdims=True) + EPS) * g


def _softplus(z):
    return jnp.maximum(z, 0.0) + jnp.log(1.0 + jnp.exp(-jnp.abs(z)))


def _sigmoid(z):
    return 1.0 / (1.0 + jnp.exp(-z))


def _split_dot(x, u):
    hi = x.astype(BF16)
    lo = (x - hi.astype(F32)).astype(BF16)
    return _dot(hi, u) + _dot(lo, u)


def _full(shape):
    nd = len(shape)
    return pl.BlockSpec(shape, lambda *_: (0,) * nd)


def _even_proj_kernel(x_ref, g_ref, w_ref, bf_ref, u_ref, q_ref, k_ref, v_ref, kb_ref, vb_ref,
                      lf_ref, c_ref, carry_ref, *, ch, tiles_per_seq):
    tm = x_ref.shape[0]
    h = _rms(x_ref[...], g_ref[...]).astype(BF16)
    ag = _dot(h, w_ref[:, 0:2 * ch])
    u_ref[...] = ag[:, :ch] * _sigmoid(ag[:, ch:])
    q_ref[...] = (_dot(h, w_ref[:, 2 * ch:3 * ch]) * HEAD_DIM ** -0.5).astype(BF16)
    k = _dot(h, w_ref[:, 3 * ch:4 * ch])
    k_ref[...] = k
    kb_ref[...] = k.astype(BF16)
    v = _dot(h, w_ref[:, 4 * ch:5 * ch])
    v_ref[...] = v
    vb_ref[...] = v.astype(BF16)
    f = _dot(h, w_ref[:, 5 * ch:5 * ch + LANES]) + bf_ref[...]
    lf = -_softplus(-f)
    nh = lf_ref.shape[1]
    lf_ref[...] = lf[:, :nh]
    if tiles_per_seq:
        @pl.when(pl.program_id(0) % tiles_per_seq == 0)
        def _():
            carry_ref[...] = jnp.zeros_like(carry_ref)
        row = lax.broadcasted_iota(jnp.int32, (tm, tm), 0)
        col = lax.broadcasted_iota(jnp.int32, (tm, tm), 1)
        tri = jnp.where(row >= col, 1.0, 0.0).astype(F32)
        c = _dot_exact(tri, lf) + carry_ref[...]
        carry_ref[...] = c[tm - 1:tm, :]
        c_ref[...] = c[:, :nh]
    else:
        c_ref[...] = lf[:, :nh]


def _even_proj(x, g, w_pad, bf_pad, *, ch, nh, seq, tm):
    m, d = x.shape
    tiles_per_seq = seq // tm if seq % tm == 0 else 0
    row = lambda n: pl.BlockSpec((tm, n), lambda i: (i, 0))
    outs = [jax.ShapeDtypeStruct((m, ch), F32), jax.ShapeDtypeStruct((m, ch), BF16),
            jax.ShapeDtypeStruct((m, ch), F32), jax.ShapeDtypeStruct((m, ch), F32),
            jax.ShapeDtypeStruct((m, ch), BF16), jax.ShapeDtypeStruct((m, ch), BF16),
            jax.ShapeDtypeStruct((m, nh), F32), jax.ShapeDtypeStruct((m, nh), F32)]
    return pl.pallas_call(
        functools.partial(_even_proj_kernel, ch=ch, tiles_per_seq=tiles_per_seq),
        grid=(m // tm,),
        in_specs=[row(d), _full((1, d)), _full(w_pad.shape), _full((1, LANES))],
        out_specs=[row(ch)] * 6 + [row(nh)] * 2,
        out_shape=outs,
        scratch_shapes=[pltpu.VMEM((1, LANES), F32)],
        compiler_params=_cparams("arbitrary"),
        name="even_proj",
    )(x, g, w_pad, bf_pad)


def _odd_proj_kernel(x_ref, g_ref, w_ref, q_ref, k_ref, v_ref, kb_ref, vb_ref, *, width):
    h = _rms(x_ref[...], g_ref[...]).astype(BF16)
    q_ref[...] = (_dot(h, w_ref[:, 0:width]) * HEAD_DIM ** -0.5).astype(BF16)
    k = _dot(h, w_ref[:, width:2 * width])
    k_ref[...] = k
    kb_ref[...] = k.astype(BF16)
    v = _dot(h, w_ref[:, 2 * width:3 * width])
    v_ref[...] = v
    vb_ref[...] = v.astype(BF16)


def _odd_proj(x, g, w, *, tm):
    m, d = x.shape
    width = w.shape[1] // 3
    row = lambda n: pl.BlockSpec((tm, n), lambda i: (i, 0))
    outs = [jax.ShapeDtypeStruct((m, width), BF16), jax.ShapeDtypeStruct((m, width), F32),
            jax.ShapeDtypeStruct((m, width), F32), jax.ShapeDtypeStruct((m, width), BF16),
            jax.ShapeDtypeStruct((m, width), BF16)]
    return pl.pallas_call(
        functools.partial(_odd_proj_kernel, width=width),
        grid=(m // tm,),
        in_specs=[row(d), _full((1, d)), _full(w.shape)],
        out_specs=[row(width)] * 5,
        out_shape=outs,
        compiler_params=_cparams("parallel"),
        name="odd_proj",
    )(x, g, w)


def _norm_matmul_kernel(x_ref, g_ref, w_ref, o_ref, *, scale):
    h = _rms(x_ref[...], g_ref[...]).astype(BF16)
    y = _dot(h, w_ref[...])
    if scale != 1.0:
        y = y * scale
    o_ref[...] = y.astype(o_ref.dtype)


def _norm_matmul(x, g, w, *, tm, out_dtype, scale=1.0):
    m, d = x.shape
    n = w.shape[1]
    return pl.pallas_call(
        functools.partial(_norm_matmul_kernel, scale=scale),
        grid=(m // tm,),
        in_specs=[pl.BlockSpec((tm, d), lambda i: (i, 0)), _full((1, d)), _full(w.shape)],
        out_specs=pl.BlockSpec((tm, n), lambda i: (i, 0)),
        out_shape=jax.ShapeDtypeStruct((m, n), out_dtype),
        compiler_params=_cparams("parallel"),
        name="norm_matmul",
    )(x, g, w)


def _linres_kernel(*refs, n_in):
    a_refs, w_refs = refs[:n_in], refs[n_in:2 * n_in]
    x_ref, g_ref, o_ref = refs[2 * n_in:]
    y = _dot(a_refs[0][...], w_refs[0][...])
    for a_ref, w_ref in zip(a_refs[1:], w_refs[1:]):
        y = y + _dot(a_ref[...], w_ref[...])
    o_ref[...] = x_ref[...] + _rms(y, g_ref[...])


def _linear_residual(acts, ws, x, g, *, tm):
    m, d = x.shape
    n_in = len(acts)
    in_specs = [pl.BlockSpec((tm, a.shape[1]), lambda i: (i, 0)) for a in acts]
    in_specs += [_full(w.shape) for w in ws]
    in_specs += [pl.BlockSpec((tm, d), lambda i: (i, 0)), _full((1, d))]
    return pl.pallas_call(
        functools.partial(_linres_kernel, n_in=n_in),
        grid=(m // tm,),
        in_specs=in_specs,
        out_specs=pl.BlockSpec((tm, d), lambda i: (i, 0)),
        out_shape=jax.ShapeDtypeStruct((m, d), F32),
        compiler_params=_cparams("parallel"),
        name="linear_residual",
    )(*acts, *ws, x, g)


def _conv_module_kernel(u_ref, st_ref, w_ref, b_ref, lg_ref, lb_ref, o_ref, full_ref, *, tt, rows):
    t = pl.program_id(1)

    @pl.when(t == 0)
    def _():
        full_ref[0:CONV_HALO, :] = st_ref[0]

    @pl.when(t > 0)
    def _():
        full_ref[0:CONV_HALO, :] = full_ref[tt:tt + CONV_HALO, :]

    full_ref[CONV_HALO:CONV_HALO + tt, :] = u_ref[0]
    first = CONV_HALO - (CONV_WIDTH - 1)
    for r0 in range(0, tt, rows):
        acc = full_ref[r0 + first:r0 + first + rows, :] * w_ref[0:1, :]
        for j in range(1, CONV_WIDTH):
            acc = acc + full_ref[r0 + first + j:r0 + first + j + rows, :] * w_ref[j:j + 1, :]
        y = acc + b_ref[...]
        mu = jnp.mean(y, axis=-1, keepdims=True)
        yc = y - mu
        var = jnp.mean(yc * yc, axis=-1, keepdims=True)
        z = yc * lax.rsqrt(var + EPS) * lg_ref[...] + lb_ref[...]
        o_ref[0, r0:r0 + rows, :] = (z * _sigmoid(z)).astype(o_ref.dtype)


def _conv_module(u, state_pad, w_pad, b, ln_g, ln_b):
    bsz, seq, ch = u.shape
    tt = min(seq, 256)
    rows = min(tt, 32)
    return pl.pallas_call(
        functools.partial(_conv_module_kernel, tt=tt, rows=rows),
        grid=(bsz, seq // tt),
        in_specs=[pl.BlockSpec((1, tt, ch), lambda b_, t: (b_, t, 0)),
                  pl.BlockSpec((1, CONV_HALO, ch), lambda b_, t: (b_, 0, 0)),
                  _full(w_pad.shape), _full((1, ch)), _full((1, ch)), _full((1, ch))],
        out_specs=pl.BlockSpec((1, tt, ch), lambda b_, t: (b_, t, 0)),
        out_shape=jax.ShapeDtypeStruct((bsz, seq, ch), BF16),
        scratch_shapes=[pltpu.VMEM((CONV_HALO + tt, ch), F32)],
        compiler_params=_cparams("parallel", "arbitrary"),
        name="conv_module",
    )(u, state_pad, w_pad, b, ln_g, ln_b)


def _key_norm_kernel(k_ref, o_ref):
    sq = jnp.square(k_ref[0].astype(F32))
    row = lax.broadcasted_iota(jnp.int32, (LANES, LANES), 0)
    col = lax.broadcasted_iota(jnp.int32, (LANES, LANES), 1)
    per_head = _split_dot(sq, jnp.where(row // HEAD_DIM == col, 1.0, 0.0).astype(BF16))
    o_ref[0, 0] = jnp.broadcast_to(jnp.max(per_head, axis=0, keepdims=True), (SUBLANES, LANES))


def _key_norms(k):
    bsz, seq, width = k.shape
    pairs = width // (2 * HEAD_DIM)
    return pl.pallas_call(
        _key_norm_kernel,
        grid=(bsz, pairs),
        in_specs=[pl.BlockSpec((1, seq, LANES), lambda b, h: (b, 0, h))],
        out_specs=pl.BlockSpec((1, 1, SUBLANES, LANES), lambda b, h: (b, h, 0, 0)),
        out_shape=jax.ShapeDtypeStruct((bsz, pairs, SUBLANES, LANES), F32),
        compiler_params=_cparams("parallel", "parallel"),
        name="key_norms",
    )(k)


def _fox_attn_kernel(q_ref, k_ref, v_ref, cc_ref, cr_ref, kn_ref, o_ref, m_ref, acc_ref, *, tq):
    qi = pl.program_id(2)
    m_ref[...] = jnp.full_like(m_ref, NEG)
    acc_ref[...] = jnp.zeros_like(acc_ref)
    cc = cc_ref[0, 0]
    lane = lax.broadcasted_iota(jnp.int32, (tq, LANES), 1)
    own = [jnp.where(lane < HEAD_DIM, 1.0, 0.0).astype(BF16), jnp.where(lane < HEAD_DIM, 0.0, 1.0).astype(BF16)]
    q_heads = [q_ref[0] * own[hh] for hh in range(2)]
    caps = []
    for hh in range(2):
        q32 = q_heads[hh].astype(F32)
        q2 = jnp.sum(q32 * q32, axis=-1, keepdims=True)
        caps.append(jnp.sqrt(q2 * kn_ref[0, 0, 0:1, hh:hh + 1]) * 1.01)

    def block(j, masked):
        ks = pl.multiple_of(j * tq, tq)
        k = k_ref[0, pl.ds(ks, tq), :]
        v = v_ref[0, pl.ds(ks, tq), :]
        for hh in range(2):
            s = _dot_nt(q_heads[hh], k)
            s = s + (cc[:, hh:hh + 1] - cr_ref[0, 0, hh:hh + 1, pl.ds(ks, tq)])
            if masked:
                row = lax.broadcasted_iota(jnp.int32, (tq, tq), 0)
                col = lax.broadcasted_iota(jnp.int32, (tq, tq), 1)
                s = jnp.where(col <= row, s, NEG)
            m_prev = m_ref[hh]
            m_new = jnp.maximum(m_prev, jnp.max(s, axis=-1, keepdims=True))
            p = jnp.exp(s - m_new)
            alpha = jnp.exp(m_prev - m_new)
            acc_ref[hh] = alpha * acc_ref[hh] + _dot(p.astype(BF16), v * own[hh] + own[1 - hh])
            m_ref[hh] = m_new

    def headroom(j):
        ks = pl.multiple_of(j * tq, tq)
        tops = []
        for hh in range(2):
            c_min = jnp.min(cr_ref[0, 0, hh:hh + 1, pl.ds(ks, tq)], axis=-1, keepdims=True)
            tops.append(jnp.max(caps[hh] + cc[:, hh:hh + 1] - c_min - m_ref[hh]))
        return jnp.maximum(tops[0], tops[1])

    def more(carry):
        jj, top = carry
        return jnp.logical_and(jj < qi, top > SB_DEAD)

    def body(carry):
        jj, _ = carry
        j = qi - 1 - jj
        block(j, False)
        return jj + 1, headroom(jnp.maximum(j - 1, 0))

    block(qi, True)
    lax.while_loop(more, body, (jnp.int32(0), headroom(jnp.maximum(qi - 1, 0))))
    acc0, acc1 = acc_ref[0], acc_ref[1]
    o_ref[0] = jnp.where(lane < HEAD_DIM, acc0 / acc0[:, HEAD_DIM:HEAD_DIM + 1],
                         acc1 / acc1[:, 0:1]).astype(o_ref.dtype)


def _fox_attention(q, k, v, c_col, c_row, *, tq):
    bsz, seq, width = q.shape
    pairs = width // (2 * HEAD_DIM)
    return pl.pallas_call(
        functools.partial(_fox_attn_kernel, tq=tq),
        grid=(bsz, pairs, seq // tq),
        in_specs=[pl.BlockSpec((1, tq, LANES), lambda b, h, i: (b, i, h)),
                  pl.BlockSpec((1, seq, LANES), lambda b, h, i: (b, 0, h)),
                  pl.BlockSpec((1, seq, LANES), lambda b, h, i: (b, 0, h)),
                  pl.BlockSpec((1, 1, tq, 2), lambda b, h, i: (b, h, i, 0)),
                  pl.BlockSpec((1, 1, 2, seq), lambda b, h, i: (b, h, 0, 0)),
                  pl.BlockSpec((1, 1, SUBLANES, LANES), lambda b, h, i: (b, h, 0, 0))],
        out_specs=pl.BlockSpec((1, tq, LANES), lambda b, h, i: (b, i, h)),
        out_shape=jax.ShapeDtypeStruct((bsz, seq, width), BF16),
        scratch_shapes=[pltpu.VMEM((2, tq, 1), F32), pltpu.VMEM((2, tq, LANES), F32)],
        compiler_params=_cparams("parallel", "parallel", "arbitrary"),
        name="fox_attention",
    )(q, k, v, c_col, c_row, _key_norms(k))


def _later_key_matrix(n, dtype):
    row = lax.broadcasted_iota(jnp.int32, (n, n), 0)
    col = lax.broadcasted_iota(jnp.int32, (n, n), 1)
    return jnp.where(row > col, 1.0, 0.0).astype(dtype)


def _sb_block(q, k, v, later, r_prev, valid, transposed=False):
    z = _dot(q, k) if transposed else _dot_nt(q, k)
    log_remain = -_softplus(z)
    log_beta = z + log_remain
    if valid is not None:
        log_remain = jnp.where(valid, log_remain, 0.0)
    stick = _split_dot(log_remain, later) + r_prev
    w = jnp.exp(log_beta + stick)
    if valid is not None:
        w = jnp.where(valid, w, 0.0)
    r_new = stick[:, 0:1] + log_remain[:, 0:1]
    w = w.astype(BF16)
    return (_dot_nt(w, v) if transposed else _dot(w, v)), r_new


def _sb_attn_kernel(q_ref, k_ref, v_ref, o_ref, r_ref, acc_ref, *, tq):
    qi = pl.program_id(2)
    later = _later_key_matrix(tq, BF16)
    heads = [slice(hh * HEAD_DIM, (hh + 1) * HEAD_DIM) for hh in range(2)]

    def block(ks, r_prev, valid):
        z = jnp.concatenate([_dot_nt(q_ref[0, :, sl], k_ref[0, pl.ds(ks, tq), sl]) for sl in heads], axis=0)
        log_remain = -_softplus(z)
        log_beta = z + log_remain
        if valid is not None:
            log_remain = jnp.where(valid, log_remain, 0.0)
        stick = _split_dot(log_remain, later) + r_prev
        w = jnp.exp(log_beta + stick)
        if valid is not None:
            w = jnp.where(valid, w, 0.0)
        w = w.astype(BF16)
        outs = [_dot(w[hh * tq:(hh + 1) * tq], v_ref[0, pl.ds(ks, tq), sl]) for hh, sl in enumerate(heads)]
        return outs, stick[:, 0:1] + log_remain[:, 0:1]

    row = lax.broadcasted_iota(jnp.int32, (2 * tq, tq), 0) % tq
    col = lax.broadcasted_iota(jnp.int32, (2 * tq, tq), 1)
    outs, r = block(pl.multiple_of(qi * tq, tq), 0.0, col < row)
    for hh in range(2):
        acc_ref[hh] = outs[hh]
    r_ref[...] = r

    def more(carry):
        jj, r_top = carry
        return jnp.logical_and(jj < qi, r_top > SB_DEAD)

    def body(carry):
        jj, _ = carry
        outs, r = block(pl.multiple_of((qi - 1 - jj) * tq, tq), r_ref[...], None)
        for hh in range(2):
            acc_ref[hh] = acc_ref[hh] + outs[hh]
        r_ref[...] = r
        return jj + 1, jnp.max(r)

    lax.while_loop(more, body, (jnp.int32(0), jnp.max(r)))
    o_ref[0] = jnp.concatenate([acc_ref[hh] for hh in range(2)], axis=-1).astype(o_ref.dtype)


def _sb_attention(q, k, v, *, tq):
    bsz, seq, width = q.shape
    pairs = width // (2 * HEAD_DIM)
    return pl.pallas_call(
        functools.partial(_sb_attn_kernel, tq=tq),
        grid=(bsz, pairs, seq // tq),
        in_specs=[pl.BlockSpec((1, tq, LANES), lambda b, h, i: (b, i, h)),
                  pl.BlockSpec((1, seq, LANES), lambda b, h, i: (b, 0, h)),
                  pl.BlockSpec((1, seq, LANES), lambda b, h, i: (b, 0, h))],
        out_specs=pl.BlockSpec((1, tq, LANES), lambda b, h, i: (b, i, h)),
        out_shape=jax.ShapeDtypeStruct((bsz, seq, width), BF16),
        scratch_shapes=[pltpu.VMEM((2 * tq, 1), F32), pltpu.VMEM((2, tq, HEAD_DIM), F32)],
        compiler_params=_cparams("parallel", "parallel", "arbitrary"),
        name="sb_attention",
    )(q, k, v)


def _mem_attn_kernel(q_ref, k_ref, v_ref, o_ref):
    outs = []
    for h in range(MEM_HEADS):
        sl = slice(h * MEM_HEAD_DIM, (h + 1) * MEM_HEAD_DIM)
        s = _dot_nt(q_ref[0, :, sl].astype(BF16), k_ref[0, :, sl].astype(BF16))
        p = jnp.exp(s - jnp.max(s, axis=-1, keepdims=True))
        o = _dot(p.astype(BF16), v_ref[0, :, sl].astype(BF16))
        outs.append(o / jnp.sum(p, axis=-1, keepdims=True))
    o_ref[0] = jnp.concatenate(outs, axis=-1).astype(o_ref.dtype)


def _mem_attention(q, mk, mv, *, tm):
    bsz, seq, width = q.shape
    n_mem = mk.shape[1]
    return pl.pallas_call(
        _mem_attn_kernel,
        grid=(bsz, seq // tm),
        in_specs=[pl.BlockSpec((1, tm, width), lambda b, i: (b, i, 0)),
                  pl.BlockSpec((1, n_mem, width), lambda b, i: (b, 0, 0)),
                  pl.BlockSpec((1, n_mem, width), lambda b, i: (b, 0, 0))],
        out_specs=pl.BlockSpec((1, tm, width), lambda b, i: (b, i, 0)),
        out_shape=jax.ShapeDtypeStruct((bsz, seq, width), BF16),
        compiler_params=_cparams("parallel", "parallel"),
        name="mem_attention",
    )(q, mk, mv)


def _ffn_kernel(*refs, seq, tiles_per_seq):
    (x_ref, g4_ref, g5_ref, wg_ref, wv_ref, cwg_ref, cwv_ref, cbg_ref, cbv_ref, wd_ref) = refs[:10]
    if tiles_per_seq:
        sg_ref, sv_ref, o_ref, og_ref, ov_ref, h_ref, acc_ref, cg_ref, cv_ref = refs[10:]
    else:
        s1g_ref, s1v_ref, s2g_ref, s2v_ref, o_ref, og_ref, ov_ref, h_ref, acc_ref = refs[10:]
    i, j = pl.program_id(0), pl.program_id(1)
    tm = x_ref.shape[0]
    tf = wg_ref.shape[1]

    @pl.when(j == 0)
    def _():
        h_ref[...] = _rms(x_ref[...], g4_ref[...]).astype(BF16)
        acc_ref[...] = jnp.zeros_like(acc_ref)

    h = h_ref[...]

    def conv(u, w_ref, b_ref, prev1, prev2):
        rmod = lax.broadcasted_iota(jnp.int32, (tm, tf), 0) % seq
        s1 = jnp.where(rmod >= 1, pltpu.roll(u, 1, 0), prev1)
        s2 = jnp.where(rmod >= 2, pltpu.roll(u, 2, 0), prev2)
        return s2 * w_ref[0:1, :] + s1 * w_ref[1:2, :] + u * w_ref[2:3, :] + b_ref[...]

    ug = _dot(h, wg_ref[...])
    uv = _dot(h, wv_ref[...])
    if tiles_per_seq:
        @pl.when(i % tiles_per_seq == 0)
        def _():
            cg_ref[j] = sg_ref[0]
            cv_ref[j] = sv_ref[0]

        pg = cg_ref[j]
        pv = cv_ref[j]
        cg_ref[j] = ug[tm - 2:tm, :]
        cv_ref[j] = uv[tm - 2:tm, :]
        og_ref[0] = ug[tm - 2:tm, :]
        ov_ref[0] = uv[tm - 2:tm, :]
        rmod = lax.broadcasted_iota(jnp.int32, (tm, tf), 0)
        cgate = conv(ug, cwg_ref, cbg_ref, pg[1:2, :], jnp.where(rmod == 0, pg[0:1, :], pg[1:2, :]))
        cval = conv(uv, cwv_ref, cbv_ref, pv[1:2, :], jnp.where(rmod == 0, pv[0:1, :], pv[1:2, :]))
    else:
        og_ref[...] = ug
        ov_ref[...] = uv
        cgate = conv(ug, cwg_ref, cbg_ref, s1g_ref[...], s2g_ref[...])
        cval = conv(uv, cwv_ref, cbv_ref, s1v_ref[...], s2v_ref[...])
    act = (cgate * _sigmoid(cgate) * cval).astype(BF16)
    acc_ref[...] += _dot(act, wd_ref[...])

    @pl.when(j == pl.num_programs(1) - 1)
    def _():
        o_ref[...] = x_ref[...] + _rms(acc_ref[...], g5_ref[...])


def _conv_ffn(x, g4, g5, w_up, conv_w, conv_b, w_down, state, *, bsz, seq, tm, tf):
    m, d = x.shape
    d_ff = w_down.shape[0]
    nf = d_ff // tf
    tiles_per_seq = seq // tm if seq % tm == 0 else 0
    kw = FFN_CONV_WIDTH
    in_specs = [pl.BlockSpec((tm, d), lambda i, j: (i, 0)), _full((1, d)), _full((1, d)),
                pl.BlockSpec((d, tf), lambda i, j: (0, j)),
                pl.BlockSpec((d, tf), lambda i, j: (0, nf + j)),
                pl.BlockSpec((kw, tf), lambda i, j: (0, j)),
                pl.BlockSpec((kw, tf), lambda i, j: (0, nf + j)),
                pl.BlockSpec((1, tf), lambda i, j: (0, j)),
                pl.BlockSpec((1, tf), lambda i, j: (0, nf + j)),
                pl.BlockSpec((tf, d), lambda i, j: (j, 0))]
    args = [x, g4, g5, w_up, w_up, conv_w, conv_w, conv_b, conv_b, w_down]
    scratch = [pltpu.VMEM((tm, d), BF16), pltpu.VMEM((tm, d), F32)]
    x_spec = pl.BlockSpec((tm, d), lambda i, j: (i, 0))
    if tiles_per_seq:
        in_specs += [pl.BlockSpec((1, kw - 1, tf), lambda i, j: (i // tiles_per_seq, 0, j)),
                     pl.BlockSpec((1, kw - 1, tf), lambda i, j: (i // tiles_per_seq, 0, nf + j))]
        args += [state, state]
        st_spec = pl.BlockSpec((1, kw - 1, tf), lambda i, j: (i, 0, j))
        out_specs = [x_spec, st_spec, st_spec]
        out_shape = [jax.ShapeDtypeStruct((m, d), F32),
                     jax.ShapeDtypeStruct((m // tm, kw - 1, d_ff), F32),
                     jax.ShapeDtypeStruct((m // tm, kw - 1, d_ff), F32)]
        scratch += [pltpu.VMEM((nf, kw - 1, tf), F32), pltpu.VMEM((nf, kw - 1, tf), F32)]
    else:
        assert tm % seq == 0 and seq >= kw - 1
        pad = seq - (kw - 1)
        s2 = jnp.pad(state, ((0, 0), (0, pad), (0, 0))).reshape(m, 2 * d_ff)
        s1 = jnp.pad(state[:, 1:], ((0, 0), (0, seq - 1), (0, 0))).reshape(m, 2 * d_ff)
        in_specs += [pl.BlockSpec((tm, tf), lambda i, j: (i, j)),
                     pl.BlockSpec((tm, tf), lambda i, j: (i, nf + j)),
                     pl.BlockSpec((tm, tf), lambda i, j: (i, j)),
                     pl.BlockSpec((tm, tf), lambda i, j: (i, nf + j))]
        args += [s1, s1, s2, s2]
        u_spec = pl.BlockSpec((tm, tf), lambda i, j: (i, j))
        out_specs = [x_spec, u_spec, u_spec]
        out_shape = [jax.ShapeDtypeStruct((m, d), F32), jax.ShapeDtypeStruct((m, d_ff), F32),
                     jax.ShapeDtypeStruct((m, d_ff), F32)]
    y, og, ov = pl.pallas_call(
        functools.partial(_ffn_kernel, seq=seq, tiles_per_seq=tiles_per_seq),
        grid=(m // tm, nf),
        in_specs=in_specs,
        out_specs=out_specs,
        out_shape=out_shape,
        scratch_shapes=scratch,
        compiler_params=_cparams("arbitrary", "arbitrary"),
        name="conv_ffn",
    )(*args)
    if tiles_per_seq:
        og = og[tiles_per_seq - 1::tiles_per_seq]
        ov = ov[tiles_per_seq - 1::tiles_per_seq]
    else:
        og = og.reshape(bsz, seq, d_ff)[:, seq - (kw - 1):]
        ov = ov.reshape(bsz, seq, d_ff)[:, seq - (kw - 1):]
    return y, jnp.concatenate([og, ov], axis=-1)


def _block_diag_queries(q, n_heads):
    t, width = q.shape
    tiled = jnp.concatenate([q.astype(F32)] * n_heads, axis=0)
    row = lax.broadcasted_iota(jnp.int32, (n_heads * t, width), 0)
    col = lax.broadcasted_iota(jnp.int32, (n_heads * t, width), 1)
    return jnp.where(row // t == col // HEAD_DIM, tiled, 0.0).astype(BF16)


def _diag_blocks(acc, t, n_heads):
    return jnp.concatenate(
        [acc[h * t:(h + 1) * t, h * HEAD_DIM:(h + 1) * HEAD_DIM] for h in range(n_heads)], axis=-1)


def _pool_key_norm_kernel(k_ref, o_ref, *, n_heads):
    @pl.when(pl.program_id(0) == 0)
    def _():
        o_ref[...] = jnp.zeros_like(o_ref)

    pages, width, page = k_ref.shape
    row = lax.broadcasted_iota(jnp.int32, (n_heads, width), 0)
    col = lax.broadcasted_iota(jnp.int32, (n_heads, width), 1)
    pick = jnp.where(col // (width // n_heads) == row, 1.0, 0.0).astype(BF16)
    top = o_ref[...]
    for p_ in range(pages):
        top = jnp.maximum(top, _dot(pick, jnp.square(k_ref[p_]).astype(BF16)))
    o_ref[...] = top


def _pool_key_norms(k_pool, n_heads):
    n_pool, width, page = k_pool.shape
    pages = max(p_ for p_ in (8, 4, 2, 1) if n_pool % p_ == 0)
    return pl.pallas_call(
        functools.partial(_pool_key_norm_kernel, n_heads=n_heads),
        grid=(n_pool // pages,),
        in_specs=[pl.BlockSpec((pages, width, page), lambda i: (i, 0, 0))],
        out_specs=pl.BlockSpec((n_heads, page), lambda i: (0, 0)),
        out_shape=jax.ShapeDtypeStruct((n_heads, page), F32),
        compiler_params=_cparams("arbitrary"),
        name="pool_key_norms",
    )(k_pool)


def _fox_decode_kernel(pt_ref, q_ref, kn_ref, vn_ref, lq_ref, kmax_ref, k_pool, v_pool, lf_pool, o_ref,
                       k_buf, v_buf, lf_buf, sems, qbd_ref, m_ref, l_ref, acc_ref, carry_ref, gq_ref,
                       *, n_pages, n_heads, t_new):
    b = pl.program_id(0)
    rows = n_heads * t_new
    later = _later_key_matrix(LANES, F32)
    lane = lax.broadcasted_iota(jnp.int32, (rows, LANES), 1)
    qpos = lax.broadcasted_iota(jnp.int32, (rows, LANES), 0) % t_new

    def page_copies(p, slot):
        page = pt_ref[b, n_pages - 1 - p]
        return (pltpu.make_async_copy(k_pool.at[page], k_buf.at[slot], sems.at[0, slot]),
                pltpu.make_async_copy(v_pool.at[page], v_buf.at[slot], sems.at[1, slot]),
                pltpu.make_async_copy(lf_pool.at[page], lf_buf.at[slot], sems.at[2, slot]))

    for cp in page_copies(0, 0):
        cp.start()

    def rows_of_heads(per_head):
        n = per_head.shape[1]
        return jnp.concatenate(
            [jnp.broadcast_to(per_head[h:h + 1, :], (t_new, n)) for h in range(n_heads)], axis=0)

    def update(s, v):
        m_prev = m_ref[...]
        m_new = jnp.maximum(m_prev, jnp.max(s, axis=-1, keepdims=True))
        p = jnp.exp(s - m_new)
        alpha = jnp.exp(m_prev - m_new)
        l_ref[...] = alpha * l_ref[...] + jnp.sum(p, axis=-1, keepdims=True)
        acc_ref[...] = alpha * acc_ref[...] + _dot_nt(p.astype(BF16), v)
        m_ref[...] = m_new

    qbd_ref[...] = _block_diag_queries(q_ref[0], n_heads)
    m_ref[...] = jnp.full_like(m_ref, NEG)
    l_ref[...] = jnp.zeros_like(l_ref)
    acc_ref[...] = jnp.zeros_like(acc_ref)
    logf_rows = lq_ref[0]
    gap = _dot_exact(logf_rows, later)
    carry_ref[...] = gap[:, 0:1] + logf_rows[:, 0:1]
    gq_ref[...] = jnp.sum(jnp.where(lane == qpos, gap, 0.0), axis=-1, keepdims=True)
    s = (gap - gq_ref[...]) + _dot(qbd_ref[...], kn_ref[0])
    update(jnp.where(lane <= qpos, s, NEG), vn_ref[0])

    q32 = qbd_ref[...].astype(F32)
    k2_max = rows_of_heads(jnp.max(kmax_ref[...], axis=-1, keepdims=True))
    cap = jnp.sqrt(jnp.sum(q32 * q32, axis=-1, keepdims=True) * k2_max) * 1.01

    def headroom():
        return jnp.max(cap + carry_ref[...] - gq_ref[...] - m_ref[...])

    def more(carry):
        p, top = carry
        return jnp.logical_and(p < n_pages, top > SB_DEAD)

    def body(carry):
        p, _ = carry
        slot = p % 2
        for cp in page_copies(p, slot):
            cp.wait()

        @pl.when(p + 1 < n_pages)
        def _():
            for cp in page_copies(p + 1, 1 - slot):
                cp.start()

        logf_rows = rows_of_heads(lf_buf[slot])
        gap = _dot_exact(logf_rows, later) + carry_ref[...]
        carry_ref[...] = gap[:, 0:1] + logf_rows[:, 0:1]
        s = (gap - gq_ref[...]) + _dot(qbd_ref[...], k_buf[slot].astype(BF16))
        update(s, v_buf[slot].astype(BF16))
        return p + 1, headroom()

    p_end, _ = lax.while_loop(more, body, (jnp.int32(0), headroom()))

    @pl.when(p_end < n_pages)
    def _():
        for cp in page_copies(p_end, p_end % 2):
            cp.wait()

    o_ref[0] = _diag_blocks(acc_ref[...] / l_ref[...], t_new, n_heads).astype(o_ref.dtype)


def _fox_decode_attention(page_table, q, k_new, v_new, k_pool, v_pool, logf_new, logf_pool):
    bsz, t_new, width = q.shape
    n_heads = width // HEAD_DIM
    n_pages = page_table.shape[1]
    page = k_pool.shape[2]
    assert page == LANES and t_new <= page
    rows = n_heads * t_new
    pad = ((0, 0), (0, 0), (0, page - t_new))
    kn, vn = jnp.pad(jnp.swapaxes(k_new, 1, 2), pad), jnp.pad(jnp.swapaxes(v_new, 1, 2), pad)
    lq = jnp.pad(jnp.repeat(jnp.swapaxes(logf_new, 1, 2), t_new, axis=1), pad)
    lf_t = jnp.swapaxes(logf_pool, 1, 2)
    per_seq = lambda shape: pl.BlockSpec(shape, lambda b, pt: (b, 0, 0))
    anywhere = pl.BlockSpec(memory_space=pl.ANY)
    return pl.pallas_call(
        functools.partial(_fox_decode_kernel, n_pages=n_pages, n_heads=n_heads, t_new=t_new),
        grid_spec=pltpu.PrefetchScalarGridSpec(
            num_scalar_prefetch=1,
            grid=(bsz,),
            in_specs=[per_seq((1, t_new, width)), per_seq((1, width, page)), per_seq((1, width, page)),
                      per_seq((1, rows, page)), pl.BlockSpec((n_heads, page), lambda b, pt: (0, 0)),
                      anywhere, anywhere, anywhere],
            out_specs=per_seq((1, t_new, width)),
            scratch_shapes=[pltpu.VMEM((2, width, page), F32), pltpu.VMEM((2, width, page), F32),
                            pltpu.VMEM((2, n_heads, page), F32), pltpu.SemaphoreType.DMA((3, 2)),
                            pltpu.VMEM((rows, width), BF16), pltpu.VMEM((rows, 1), F32),
                            pltpu.VMEM((rows, 1), F32), pltpu.VMEM((rows, width), F32),
                            pltpu.VMEM((rows, 1), F32), pltpu.VMEM((rows, 1), F32)]),
        out_shape=jax.ShapeDtypeStruct((bsz, t_new, width), F32),
        compiler_params=_cparams("arbitrary"),
        name="fox_decode_attention",
    )(page_table, q, kn, vn, lq, _pool_key_norms(k_pool, n_heads), k_pool, v_pool, lf_t)


def _sb_decode_kernel(pt_ref, q_ref, kn_ref, vn_ref, k_pool, v_pool, o_ref,
                      k_buf, v_buf, sems, qbd_ref, r_ref, acc_ref, *, n_pages, n_heads, t_new):
    b = pl.program_id(0)
    rows = n_heads * t_new
    later = _later_key_matrix(LANES, BF16)

    def page_copies(p, slot):
        page = pt_ref[b, n_pages - 1 - p]
        return (pltpu.make_async_copy(k_pool.at[page], k_buf.at[slot], sems.at[0, slot]),
                pltpu.make_async_copy(v_pool.at[page], v_buf.at[slot], sems.at[1, slot]))

    for cp in page_copies(0, 0):
        cp.start()

    qbd_ref[...] = _block_diag_queries(q_ref[0], n_heads)
    lane = lax.broadcasted_iota(jnp.int32, (rows, LANES), 1)
    qpos = lax.broadcasted_iota(jnp.int32, (rows, LANES), 0) % t_new
    o, r = _sb_block(qbd_ref[...], kn_ref[0], vn_ref[0], later, 0.0, lane < qpos, transposed=True)
    acc_ref[...] = o
    r_ref[...] = r

    def more(carry):
        p, r_top = carry
        return jnp.logical_and(p < n_pages, r_top > SB_DEAD)

    def body(carry):
        p, _ = carry
        slot = p % 2
        for cp in page_copies(p, slot):
            cp.wait()

        @pl.when(p + 1 < n_pages)
        def _():
            for cp in page_copies(p + 1, 1 - slot):
                cp.start()

        o, r = _sb_block(qbd_ref[...], k_buf[slot].astype(BF16), v_buf[slot].astype(BF16),
                         later, r_ref[...], None, transposed=True)
        acc_ref[...] += o
        r_ref[...] = r
        return p + 1, jnp.max(r)

    p_end, _ = lax.while_loop(more, body, (jnp.int32(0), jnp.max(r_ref[...])))

    @pl.when(p_end < n_pages)
    def _():
        for cp in page_copies(p_end, p_end % 2):
            cp.wait()

    o_ref[0] = _diag_blocks(acc_ref[...], t_new, n_heads).astype(o_ref.dtype)


def _sb_decode_attention(page_table, q, k_new, v_new, k_pool, v_pool):
    bsz, t_new, width = q.shape
    n_heads = width // HEAD_DIM
    n_pages = page_table.shape[1]
    page = k_pool.shape[2]
    assert page == LANES and t_new <= page
    rows = n_heads * t_new
    pad = ((0, 0), (0, 0), (0, page - t_new))
    kn, vn = jnp.pad(jnp.swapaxes(k_new, 1, 2), pad), jnp.pad(jnp.swapaxes(v_new, 1, 2), pad)
    per_seq = lambda shape: pl.BlockSpec(shape, lambda b, pt: (b, 0, 0))
    return pl.pallas_call(
        functools.partial(_sb_decode_kernel, n_pages=n_pages, n_heads=n_heads, t_new=t_new),
        grid_spec=pltpu.PrefetchScalarGridSpec(
            num_scalar_prefetch=1,
            grid=(bsz,),
            in_specs=[per_seq((1, t_new, width)), per_seq((1, width, page)), per_seq((1, width, page)),
                      pl.BlockSpec(memory_space=pl.ANY), pl.BlockSpec(memory_space=pl.ANY)],
            out_specs=per_seq((1, t_new, width)),
            scratch_shapes=[pltpu.VMEM((2, width, page), F32), pltpu.VMEM((2, width, page), F32),
                            pltpu.SemaphoreType.DMA((2, 2)),
                            pltpu.VMEM((rows, width), BF16), pltpu.VMEM((rows, 1), F32),
                            pltpu.VMEM((rows, width), F32)]),
        out_shape=jax.ShapeDtypeStruct((bsz, t_new, width), F32),
        compiler_params=_cparams("arbitrary"),
        name="sb_decode_attention",
    )(page_table, q, kn, vn, k_pool, v_pool)


def _prep_weights(p):
    w = {}
    n_even = p["even_w_in"].shape[0]
    ch = p["conv_w"].shape[2]
    even_in = p["even_w_in"].shape[2]
    n_fox = even_in - 5 * ch
    w["even_w_in"] = jnp.pad(p["even_w_in"], ((0, 0), (0, 0), (0, LANES - n_fox))).astype(BF16)
    w["even_b_f"] = jnp.pad(p["even_b_f"], ((0, 0), (0, LANES - n_fox))).reshape(n_even, 1, LANES)
    w["conv_w"] = jnp.pad(p["conv_w"], ((0, 0), (0, CONV_HALO - CONV_WIDTH), (0, 0)))
    for name in ("even_w_out", "sb_w_in", "sb_w_out", "mem_wq", "mem_wo", "ffn_w_up", "ffn_w_down"):
        w[name] = p[name].astype(BF16)
    w["mem_wkv"] = jnp.concatenate([p["mem_wk"], p["mem_wv"]], axis=-1).astype(BF16)
    return w


def _pages_transposed(pool):
    n, page, heads, dim = pool.shape
    return jnp.transpose(pool, (0, 2, 3, 1)).reshape(n, heads * dim, page)


def _run_trunk(x, mem_k, mem_v, conv_state, ffn_state, fox_cache, sb_cache, page_table, p, w):
    bsz, seq, d = x.shape
    m = bsz * seq
    depth = p["norm_g"].shape[0]
    ch = p["conv_w"].shape[2]
    n_fox = p["even_b_f"].shape[1]
    prompt = fox_cache is None
    tm = min(m, 256)
    tq = min(seq, 256)
    x = x.reshape(m, d)
    fox_k, fox_v, fox_logf, conv_new, sb_k, sb_v, ffn_new = [], [], [], [], [], [], []
    for layer in range(depth):
        g = p["norm_g"][layer].reshape(6, 1, d)
        i = layer // 2
        if layer % 2 == 0:
            u, q, k, v, kb, vb, logf, cum = _even_proj(
                x, g[0], w["even_w_in"][i], w["even_b_f"][i], ch=ch, nh=n_fox, seq=seq, tm=tm)
            u3 = u.reshape(bsz, seq, ch)
            state = conv_state[i]
            halo = CONV_WIDTH - 1
            state_pad = jnp.pad(state, ((0, 0), (CONV_HALO - halo, 0), (0, 0)))
            a_out = _conv_module(u3, state_pad, w["conv_w"][i], p["conv_b"][i].reshape(1, ch),
                                 p["conv_ln_g"][i].reshape(1, ch), p["conv_ln_b"][i].reshape(1, ch))
            conv_new.append(jnp.concatenate([state, u3], axis=1)[:, seq:])
            q3, kb3, vb3 = (a.reshape(bsz, seq, ch) for a in (q, kb, vb))
            if prompt:
                c4 = cum.reshape(bsz, seq, n_fox // 2, 2)
                o = _fox_attention(q3, kb3, vb3, jnp.transpose(c4, (0, 2, 1, 3)),
                                   jnp.transpose(c4, (0, 2, 3, 1)), tq=tq)
            else:
                pool = lambda c: _pages_transposed(c[i])
                o = _fox_decode_attention(page_table, q3.astype(F32), kb3, vb3, pool(fox_cache[0]),
                                          pool(fox_cache[1]), logf.reshape(bsz, seq, n_fox), fox_cache[2][i])
            x = _linear_residual([a_out.reshape(m, ch), o.reshape(m, ch).astype(BF16)],
                                 [w["even_w_out"][i][:ch], w["even_w_out"][i][ch:]], x, g[1], tm=tm)
            fox_k.append(k.reshape(bsz, seq, n_fox, HEAD_DIM))
            fox_v.append(v.reshape(bsz, seq, n_fox, HEAD_DIM))
            fox_logf.append(logf.reshape(bsz, seq, n_fox))
        else:
            q, k, v, kb, vb = _odd_proj(x, g[0], w["sb_w_in"][i], tm=tm)
            width = q.shape[1]
            q3, kb3, vb3 = (a.reshape(bsz, seq, width) for a in (q, kb, vb))
            if prompt:
                o = _sb_attention(q3, kb3, vb3, tq=tq)
            else:
                pool = lambda c: _pages_transposed(c[i])
                o = _sb_decode_attention(page_table, q3.astype(F32), kb3, vb3, pool(sb_cache[0]), pool(sb_cache[1]))
            x = _linear_residual([o.reshape(m, width).astype(BF16)], [w["sb_w_out"][i]], x, g[1], tm=tm)
            sb_k.append(k.reshape(bsz, seq, width // HEAD_DIM, HEAD_DIM))
            sb_v.append(v.reshape(bsz, seq, width // HEAD_DIM, HEAD_DIM))
        qm = _norm_matmul(x, g[2], w["mem_wq"][layer], tm=tm, out_dtype=BF16 if seq % 16 == 0 else F32,
                          scale=MEM_HEAD_DIM ** -0.5)
        mw = qm.shape[1]
        om = _mem_attention(qm.reshape(bsz, seq, mw), mem_k[layer], mem_v[layer], tm=min(seq, 512))
        x = _linear_residual([om.reshape(m, mw)], [w["mem_wo"][layer]], x, g[3], tm=tm)
        d_ff2 = p["ffn_w_up"].shape[2]
        x, buf = _conv_ffn(x, g[4], g[5], w["ffn_w_up"][layer], p["ffn_conv_w"][layer],
                           p["ffn_conv_b"][layer].reshape(1, d_ff2), w["ffn_w_down"][layer],
                           ffn_state[layer], bsz=bsz, seq=seq, tm=min(m, 512), tf=256)
        ffn_new.append(buf)
    return (x.reshape(bsz, seq, d), jnp.stack(fox_k), jnp.stack(fox_v), jnp.stack(fox_logf),
            jnp.stack(conv_new), jnp.stack(sb_k), jnp.stack(sb_v), jnp.stack(ffn_new))


def kernel(x_prompt, x_sample, cache_fox_k, cache_fox_v, cache_fox_logf, state_conv, cache_sb_k, cache_sb_v, cache_mem_k, cache_mem_v, state_ffn_conv, page_table, mem_prompt, norm_g, even_w_in, even_b_f, conv_w, conv_b, conv_ln_g, conv_ln_b, even_w_out, sb_w_in, sb_w_out, mem_norm_g, mem_wq, mem_wk, mem_wv, mem_wo, ffn_w_up, ffn_conv_w, ffn_conv_b, ffn_w_down):
    p = dict(norm_g=norm_g, even_w_in=even_w_in, even_b_f=even_b_f, conv_w=conv_w, conv_b=conv_b,
             conv_ln_g=conv_ln_g, conv_ln_b=conv_ln_b, even_w_out=even_w_out, sb_w_in=sb_w_in,
             sb_w_out=sb_w_out, mem_wq=mem_wq, mem_wk=mem_wk, mem_wv=mem_wv, mem_wo=mem_wo,
             ffn_w_up=ffn_w_up, ffn_conv_w=ffn_conv_w, ffn_conv_b=ffn_conv_b, ffn_w_down=ffn_w_down)
    w = _prep_weights(p)
    depth, d = norm_g.shape[0], norm_g.shape[2]
    bsz, n_mem = mem_prompt.shape[0], mem_prompt.shape[1]
    mem_width = mem_wq.shape[2]
    n_even = even_w_in.shape[0]
    ch = conv_w.shape[2]

    mem_flat = mem_prompt.reshape(bsz * n_mem, d)
    mem_kv = [_norm_matmul(mem_flat, mem_norm_g[l].reshape(1, d), w["mem_wkv"][l], tm=256, out_dtype=F32)
              for l in range(depth)]
    mem_k_prompt = jnp.stack([kv[:, :mem_width].reshape(bsz, n_mem, mem_width) for kv in mem_kv])
    mem_v_prompt = jnp.stack([kv[:, mem_width:].reshape(bsz, n_mem, mem_width) for kv in mem_kv])
    conv_zero = jnp.zeros((n_even, bsz, CONV_WIDTH - 1, ch), F32)
    ffn_zero = jnp.zeros((depth, bsz, FFN_CONV_WIDTH - 1, ffn_w_up.shape[2]), F32)
    (y_prompt, fox_k_prompt, fox_v_prompt, fox_logf_prompt, conv_state_prompt,
     sb_k_prompt, sb_v_prompt, ffn_state_prompt) = _run_trunk(
        x_prompt, mem_k_prompt, mem_v_prompt, conv_zero, ffn_zero, None, None, None, p, w)

    dbsz = x_sample.shape[0]
    cmk = cache_mem_k.reshape(depth, dbsz, n_mem, mem_width)
    cmv = cache_mem_v.reshape(depth, dbsz, n_mem, mem_width)
    (y_sample, fox_k_sample, fox_v_sample, fox_logf_sample, conv_state_sample,
     sb_k_sample, sb_v_sample, ffn_state_sample) = _run_trunk(
        x_sample, cmk, cmv, state_conv, state_ffn_conv,
        (cache_fox_k, cache_fox_v, cache_fox_logf), (cache_sb_k, cache_sb_v), page_table, p, w)

    mem_shape = (depth, bsz, n_mem, MEM_HEADS, MEM_HEAD_DIM)
    return (y_prompt, y_sample, fox_k_prompt, fox_v_prompt, fox_logf_prompt, conv_state_prompt,
            sb_k_prompt, sb_v_prompt, mem_k_prompt.reshape(mem_shape), mem_v_prompt.reshape(mem_shape),
            ffn_state_prompt, fox_k_sample, fox_v_sample, fox_logf_sample, conv_state_sample,
            sb_k_sample, sb_v_sample, ffn_state_sample)
```

```python
import functools

import jax
import jax.numpy as jnp
from jax import lax
from jax.experimental import pallas as pl
from jax.experimental.pallas import tpu as pltpu

F32 = jnp.float32
BF16 = jnp.bfloat16

EPS = 1e-6
NEG = -1e30
SB_DEAD = -105.0
HEAD_DIM = 64
MEM_HEADS = 4
MEM_HEAD_DIM = 128
CONV_WIDTH = 31
FFN_CONV_WIDTH = 3
LANES = 128
SUBLANES = 8
CONV_HALO = 32
VMEM_LIMIT = 48 * 1024 * 1024


def _cparams(*sem):
    return pltpu.CompilerParams(dimension_semantics=sem, vmem_limit_bytes=VMEM_LIMIT)


def _dot(a, b):
    return jnp.dot(a, b, preferred_element_type=F32)


def _dot_nt(a, b):
    return lax.dot_general(a, b, (((1,), (1,)), ((), ())), preferred_element_type=F32)


def _dot_exact(a, b):
    return jnp.dot(a, b, preferred_element_type=F32, precision=lax.Precision.HIGHEST)


def _rms(x, g):
    return x * lax.rsqrt(jnp.mean(x * x, axis=-1, keepdims=True) + EPS) * g


def _softplus(z):
    return jnp.maximum(z, 0.0) + jnp.log(1.0 + jnp.exp(-jnp.abs(z)))


def _sigmoid(z):
    return 1.0 / (1.0 + jnp.exp(-z))


def _split_dot(x, u):
    hi = x.astype(BF16)
    lo = (x - hi.astype(F32)).astype(BF16)
    return _dot(hi, u) + _dot(lo, u)


def _full(shape):
    nd = len(shape)
    return pl.BlockSpec(shape, lambda *_: (0,) * nd)


def _even_proj_kernel(x_ref, g_ref, w_ref, bf_ref, u_ref, q_ref, k_ref, v_ref, kb_ref, vb_ref,
                      lf_ref, c_ref, carry_ref, *, ch, tiles_per_seq):
    tm = x_ref.shape[0]
    h = _rms(x_ref[...], g_ref[...]).astype(BF16)
    ag = _dot(h, w_ref[:, 0:2 * ch])
    u_ref[...] = ag[:, :ch] * _sigmoid(ag[:, ch:])
    q_ref[...] = (_dot(h, w_ref[:, 2 * ch:3 * ch]) * HEAD_DIM ** -0.5).astype(BF16)
    k = _dot(h, w_ref[:, 3 * ch:4 * ch])
    k_ref[...] = k
    kb_ref[...] = k.astype(BF16)
    v = _dot(h, w_ref[:, 4 * ch:5 * ch])
    v_ref[...] = v
    vb_ref[...] = v.astype(BF16)
    f = _dot(h, w_ref[:, 5 * ch:5 * ch + LANES]) + bf_ref[...]
    lf = -_softplus(-f)
    nh = lf_ref.shape[1]
    lf_ref[...] = lf[:, :nh]
    if tiles_per_seq:
        @pl.when(pl.program_id(0) % tiles_per_seq == 0)
        def _():
            carry_ref[...] = jnp.zeros_like(carry_ref)
        row = lax.broadcasted_iota(jnp.int32, (tm, tm), 0)
        col = lax.broadcasted_iota(jnp.int32, (tm, tm), 1)
        tri = jnp.where(row >= col, 1.0, 0.0).astype(F32)
        c = _dot_exact(tri, lf) + carry_ref[...]
        carry_ref[...] = c[tm - 1:tm, :]
        c_ref[...] = c[:, :nh]
    else:
        c_ref[...] = lf[:, :nh]


def _even_proj(x, g, w_pad, bf_pad, *, ch, nh, seq, tm):
    m, d = x.shape
    tiles_per_seq = seq // tm if seq % tm == 0 else 0
    row = lambda n: pl.BlockSpec((tm, n), lambda i: (i, 0))
    outs = [jax.ShapeDtypeStruct((m, ch), F32), jax.ShapeDtypeStruct((m, ch), BF16),
            jax.ShapeDtypeStruct((m, ch), F32), jax.ShapeDtypeStruct((m, ch), F32),
            jax.ShapeDtypeStruct((m, ch), BF16), jax.ShapeDtypeStruct((m, ch), BF16),
            jax.ShapeDtypeStruct((m, nh), F32), jax.ShapeDtypeStruct((m, nh), F32)]
    return pl.pallas_call(
        functools.partial(_even_proj_kernel, ch=ch, tiles_per_seq=tiles_per_seq),
        grid=(m // tm,),
        in_specs=[row(d), _full((1, d)), _full(w_pad.shape), _full((1, LANES))],
        out_specs=[row(ch)] * 6 + [row(nh)] * 2,
        out_shape=outs,
        scratch_shapes=[pltpu.VMEM((1, LANES), F32)],
        compiler_params=_cparams("arbitrary"),
        name="even_proj",
    )(x, g, w_pad, bf_pad)


def _odd_proj_kernel(x_ref, g_ref, w_ref, q_ref, k_ref, v_ref, kb_ref, vb_ref, *, width):
    h = _rms(x_ref[...], g_ref[...]).astype(BF16)
    q_ref[...] = (_dot(h, w_ref[:, 0:width]) * HEAD_DIM ** -0.5).astype(BF16)
    k = _dot(h, w_ref[:, width:2 * width])
    k_ref[...] = k
    kb_ref[...] = k.astype(BF16)
    v = _dot(h, w_ref[:, 2 * width:3 * width])
    v_ref[...] = v
    vb_ref[...] = v.astype(BF16)


def _odd_proj(x, g, w, *, tm):
    m, d = x.shape
    width = w.shape[1] // 3
    row = lambda n: pl.BlockSpec((tm, n), lambda i: (i, 0))
    outs = [jax.ShapeDtypeStruct((m, width), BF16), jax.ShapeDtypeStruct((m, width), F32),
            jax.ShapeDtypeStruct((m, width), F32), jax.ShapeDtypeStruct((m, width), BF16),
            jax.ShapeDtypeStruct((m, width), BF16)]
    return pl.pallas_call(
        functools.partial(_odd_proj_kernel, width=width),
        grid=(m // tm,),
        in_specs=[row(d), _full((1, d)), _full(w.shape)],
        out_specs=[row(width)] * 5,
        out_shape=outs,
        compiler_params=_cparams("parallel"),
        name="odd_proj",
    )(x, g, w)


def _norm_matmul_kernel(x_ref, g_ref, w_ref, o_ref, *, scale):
    h = _rms(x_ref[...], g_ref[...]).astype(BF16)
    y = _dot(h, w_ref[...])
    if scale != 1.0:
        y = y * scale
    o_ref[...] = y.astype(o_ref.dtype)


def _norm_matmul(x, g, w, *, tm, out_dtype, scale=1.0):
    m, d = x.shape
    n = w.shape[1]
    return pl.pallas_call(
        functools.partial(_norm_matmul_kernel, scale=scale),
        grid=(m // tm,),
        in_specs=[pl.BlockSpec((tm, d), lambda i: (i, 0)), _full((1, d)), _full(w.shape)],
        out_specs=pl.BlockSpec((tm, n), lambda i: (i, 0)),
        out_shape=jax.ShapeDtypeStruct((m, n), out_dtype),
        compiler_params=_cparams("parallel"),
        name="norm_matmul",
    )(x, g, w)


def _linres_kernel(*refs, n_in):
    a_refs, w_refs = refs[:n_in], refs[n_in:2 * n_in]
    x_ref, g_ref, o_ref = refs[2 * n_in:]
    y = _dot(a_refs[0][...], w_refs[0][...])
    for a_ref, w_ref in zip(a_refs[1:], w_refs[1:]):
        y = y + _dot(a_ref[...], w_ref[...])
    o_ref[...] = x_ref[...] + _rms(y, g_ref[...])


def _linear_residual(acts, ws, x, g, *, tm):
    m, d = x.shape
    n_in = len(acts)
    in_specs = [pl.BlockSpec((tm, a.shape[1]), lambda i: (i, 0)) for a in acts]
    in_specs += [_full(w.shape) for w in ws]
    in_specs += [pl.BlockSpec((tm, d), lambda i: (i, 0)), _full((1, d))]
    return pl.pallas_call(
        functools.partial(_linres_kernel, n_in=n_in),
        grid=(m // tm,),
        in_specs=in_specs,
        out_specs=pl.BlockSpec((tm, d), lambda i: (i, 0)),
        out_shape=jax.ShapeDtypeStruct((m, d), F32),
        compiler_params=_cparams("parallel"),
        name="linear_residual",
    )(*acts, *ws, x, g)


def _conv_module_kernel(u_ref, st_ref, w_ref, b_ref, lg_ref, lb_ref, o_ref, full_ref, *, tt, rows):
    t = pl.program_id(1)

    @pl.when(t == 0)
    def _():
        full_ref[0:CONV_HALO, :] = st_ref[0]

    @pl.when(t > 0)
    def _():
        full_ref[0:CONV_HALO, :] = full_ref[tt:tt + CONV_HALO, :]

    full_ref[CONV_HALO:CONV_HALO + tt, :] = u_ref[0]
    first = CONV_HALO - (CONV_WIDTH - 1)
    for r0 in range(0, tt, rows):
        acc = full_ref[r0 + first:r0 + first + rows, :] * w_ref[0:1, :]
        for j in range(1, CONV_WIDTH):
            acc = acc + full_ref[r0 + first + j:r0 + first + j + rows, :] * w_ref[j:j + 1, :]
        y = acc + b_ref[...]
        mu = jnp.mean(y, axis=-1, keepdims=True)
        yc = y - mu
        var = jnp.mean(yc * yc, axis=-1, keepdims=True)
        z = yc * lax.rsqrt(var + EPS) * lg_ref[...] + lb_ref[...]
        o_ref[0, r0:r0 + rows, :] = (z * _sigmoid(z)).astype(o_ref.dtype)


def _conv_module(u, state_pad, w_pad, b, ln_g, ln_b):
    bsz, seq, ch = u.shape
    tt = min(seq, 256)
    rows = min(tt, 32)
    return pl.pallas_call(
        functools.partial(_conv_module_kernel, tt=tt, rows=rows),
        grid=(bsz, seq // tt),
        in_specs=[pl.BlockSpec((1, tt, ch), lambda b_, t: (b_, t, 0)),
                  pl.BlockSpec((1, CONV_HALO, ch), lambda b_, t: (b_, 0, 0)),
                  _full(w_pad.shape), _full((1, ch)), _full((1, ch)), _full((1, ch))],
        out_specs=pl.BlockSpec((1, tt, ch), lambda b_, t: (b_, t, 0)),
        out_shape=jax.ShapeDtypeStruct((bsz, seq, ch), BF16),
        scratch_shapes=[pltpu.VMEM((CONV_HALO + tt, ch), F32)],
        compiler_params=_cparams("parallel", "arbitrary"),
        name="conv_module",
    )(u, state_pad, w_pad, b, ln_g, ln_b)


def _key_norm_kernel(k_ref, o_ref):
    sq = jnp.square(k_ref[0].astype(F32))
    row = lax.broadcasted_iota(jnp.int32, (LANES, LANES), 0)
    col = lax.broadcasted_iota(jnp.int32, (LANES, LANES), 1)
    per_head = _split_dot(sq, jnp.where(row // HEAD_DIM == col, 1.0, 0.0).astype(BF16))
    o_ref[0, 0] = jnp.broadcast_to(jnp.max(per_head, axis=0, keepdims=True), (SUBLANES, LANES))


def _key_norms(k):
    bsz, seq, width = k.shape
    pairs = width // (2 * HEAD_DIM)
    return pl.pallas_call(
        _key_norm_kernel,
        grid=(bsz, pairs),
        in_specs=[pl.BlockSpec((1, seq, LANES), lambda b, h: (b, 0, h))],
        out_specs=pl.BlockSpec((1, 1, SUBLANES, LANES), lambda b, h: (b, h, 0, 0)),
        out_shape=jax.ShapeDtypeStruct((bsz, pairs, SUBLANES, LANES), F32),
        compiler_params=_cparams("parallel", "parallel"),
        name="key_norms",
    )(k)


def _fox_attn_kernel(q_ref, k_ref, v_ref, cc_ref, cr_ref, kn_ref, o_ref, m_ref, acc_ref, *, tq):
    qi = pl.program_id(2)
    m_ref[...] = jnp.full_like(m_ref, NEG)
    acc_ref[...] = jnp.zeros_like(acc_ref)
    cc = cc_ref[0, 0]
    lane = lax.broadcasted_iota(jnp.int32, (tq, LANES), 1)
    own = [jnp.where(lane < HEAD_DIM, 1.0, 0.0).astype(BF16), jnp.where(lane < HEAD_DIM, 0.0, 1.0).astype(BF16)]
    q_heads = [q_ref[0] * own[hh] for hh in range(2)]
    caps = []
    for hh in range(2):
        q32 = q_heads[hh].astype(F32)
        q2 = jnp.sum(q32 * q32, axis=-1, keepdims=True)
        caps.append(jnp.sqrt(q2 * kn_ref[0, 0, 0:1, hh:hh + 1]) * 1.01)

    def block(j, masked):
        ks = pl.multiple_of(j * tq, tq)
        k = k_ref[0, pl.ds(ks, tq), :]
        v = v_ref[0, pl.ds(ks, tq), :]
        for hh in range(2):
            s = _dot_nt(q_heads[hh], k)
            s = s + (cc[:, hh:hh + 1] - cr_ref[0, 0, hh:hh + 1, pl.ds(ks, tq)])
            if masked:
                row = lax.broadcasted_iota(jnp.int32, (tq, tq), 0)
                col = lax.broadcasted_iota(jnp.int32, (tq, tq), 1)
                s = jnp.where(col <= row, s, NEG)
            m_prev = m_ref[hh]
            m_new = jnp.maximum(m_prev, jnp.max(s, axis=-1, keepdims=True))
            p = jnp.exp(s - m_new)
            alpha = jnp.exp(m_prev - m_new)
            acc_ref[hh] = alpha * acc_ref[hh] + _dot(p.astype(BF16), v * own[hh] + own[1 - hh])
            m_ref[hh] = m_new

    def headroom(j):
        ks = pl.multiple_of(j * tq, tq)
        tops = []
        for hh in range(2):
            c_min = jnp.min(cr_ref[0, 0, hh:hh + 1, pl.ds(ks, tq)], axis=-1, keepdims=True)
            tops.append(jnp.max(caps[hh] + cc[:, hh:hh + 1] - c_min - m_ref[hh]))
        return jnp.maximum(tops[0], tops[1])

    def more(carry):
        jj, top = carry
        return jnp.logical_and(jj < qi, top > SB_DEAD)

    def body(carry):
        jj, _ = carry
        j = qi - 1 - jj
        block(j, False)
        return jj + 1, headroom(jnp.maximum(j - 1, 0))

    block(qi, True)
    lax.while_loop(more, body, (jnp.int32(0), headroom(jnp.maximum(qi - 1, 0))))
    acc0, acc1 = acc_ref[0], acc_ref[1]
    o_ref[0] = jnp.where(lane < HEAD_DIM, acc0 / acc0[:, HEAD_DIM:HEAD_DIM + 1],
                         acc1 / acc1[:, 0:1]).astype(o_ref.dtype)


def _fox_attention(q, k, v, c_col, c_row, *, tq):
    bsz, seq, width = q.shape
    pairs = width // (2 * HEAD_DIM)
    return pl.pallas_call(
        functools.partial(_fox_attn_kernel, tq=tq),
        grid=(bsz, pairs, seq // tq),
        in_specs=[pl.BlockSpec((1, tq, LANES), lambda b, h, i: (b, i, h)),
                  pl.BlockSpec((1, seq, LANES), lambda b, h, i: (b, 0, h)),
                  pl.BlockSpec((1, seq, LANES), lambda b, h, i: (b, 0, h)),
                  pl.BlockSpec((1, 1, tq, 2), lambda b, h, i: (b, h, i, 0)),
                  pl.BlockSpec((1, 1, 2, seq), lambda b, h, i: (b, h, 0, 0)),
                  pl.BlockSpec((1, 1, SUBLANES, LANES), lambda b, h, i: (b, h, 0, 0))],
        out_specs=pl.BlockSpec((1, tq, LANES), lambda b, h, i: (b, i, h)),
        out_shape=jax.ShapeDtypeStruct((bsz, seq, width), BF16),
        scratch_shapes=[pltpu.VMEM((2, tq, 1), F32), pltpu.VMEM((2, tq, LANES), F32)],
        compiler_params=_cparams("parallel", "parallel", "arbitrary"),
        name="fox_attention",
    )(q, k, v, c_col, c_row, _key_norms(k))


def _later_key_matrix(n, dtype):
    row = lax.broadcasted_iota(jnp.int32, (n, n), 0)
    col = lax.broadcasted_iota(jnp.int32, (n, n), 1)
    return jnp.where(row > col, 1.0, 0.0).astype(dtype)


def _sb_block(q, k, v, later, r_prev, valid, transposed=False):
    z = _dot(q, k) if transposed else _dot_nt(q, k)
    log_remain = -_softplus(z)
    log_beta = z + log_remain
    if valid is not None:
        log_remain = jnp.where(valid, log_remain, 0.0)
    stick = _split_dot(log_remain, later) + r_prev
    w = jnp.exp(log_beta + stick)
    if valid is not None:
        w = jnp.where(valid, w, 0.0)
    r_new = stick[:, 0:1] + log_remain[:, 0:1]
    w = w.astype(BF16)
    return (_dot_nt(w, v) if transposed else _dot(w, v)), r_new


def _sb_attn_kernel(q_ref, k_ref, v_ref, o_ref, r_ref, acc_ref, *, tq):
    qi = pl.program_id(2)
    later = _later_key_matrix(tq, BF16)
    heads = [slice(hh * HEAD_DIM, (hh + 1) * HEAD_DIM) for hh in range(2)]

    def block(ks, r_prev, valid):
        z = jnp.concatenate([_dot_nt(q_ref[0, :, sl], k_ref[0, pl.ds(ks, tq), sl]) for sl in heads], axis=0)
        log_remain = -_softplus(z)
        log_beta = z + log_remain
        if valid is not None:
            log_remain = jnp.where(valid, log_remain, 0.0)
        stick = _split_dot(log_remain, later) + r_prev
        w = jnp.exp(log_beta + stick)
        if valid is not None:
            w = jnp.where(valid, w, 0.0)
        w = w.astype(BF16)
        outs = [_dot(w[hh * tq:(hh + 1) * tq], v_ref[0, pl.ds(ks, tq), sl]) for hh, sl in enumerate(heads)]
        return outs, stick[:, 0:1] + log_remain[:, 0:1]

    row = lax.broadcasted_iota(jnp.int32, (2 * tq, tq), 0) % tq
    col = lax.broadcasted_iota(jnp.int32, (2 * tq, tq), 1)
    outs, r = block(pl.multiple_of(qi * tq, tq), 0.0, col < row)
    for hh in range(2):
        acc_ref[hh] = outs[hh]
    r_ref[...] = r

    def more(carry):
        jj, r_top = carry
        return jnp.logical_and(jj < qi, r_top > SB_DEAD)

    def body(carry):
        jj, _ = carry
        outs, r = block(pl.multiple_of((qi - 1 - jj) * tq, tq), r_ref[...], None)
        for hh in range(2):
            acc_ref[hh] = acc_ref[hh] + outs[hh]
        r_ref[...] = r
        return jj + 1, jnp.max(r)

    lax.while_loop(more, body, (jnp.int32(0), jnp.max(r)))
    o_ref[0] = jnp.concatenate([acc_ref[hh] for hh in range(2)], axis=-1).astype(o_ref.dtype)


def _sb_attention(q, k, v, *, tq):
    bsz, seq, width = q.shape
    pairs = width // (2 * HEAD_DIM)
    return pl.pallas_call(
        functools.partial(_sb_attn_kernel, tq=tq),
        grid=(bsz, pairs, seq // tq),
        in_specs=[pl.BlockSpec((1, tq, LANES), lambda b, h, i: (b, i, h)),
                  pl.BlockSpec((1, seq, LANES), lambda b, h, i: (b, 0, h)),
                  pl.BlockSpec((1, seq, LANES), lambda b, h, i: (b, 0, h))],
        out_specs=pl.BlockSpec((1, tq, LANES), lambda b, h, i: (b, i, h)),
        out_shape=jax.ShapeDtypeStruct((bsz, seq, width), BF16),
        scratch_shapes=[pltpu.VMEM((2 * tq, 1), F32), pltpu.VMEM((2, tq, HEAD_DIM), F32)],
        compiler_params=_cparams("parallel", "parallel", "arbitrary"),
        name="sb_attention",
    )(q, k, v)


def _mem_attn_kernel(q_ref, k_ref, v_ref, o_ref):
    outs = []
    for h in range(MEM_HEADS):
        sl = slice(h * MEM_HEAD_DIM, (h + 1) * MEM_HEAD_DIM)
        s = _dot_nt(q_ref[0, :, sl].astype(BF16), k_ref[0, :, sl].astype(BF16))
        p = jnp.exp(s - jnp.max(s, axis=-1, keepdims=True))
        o = _dot(p.astype(BF16), v_ref[0, :, sl].astype(BF16))
        outs.append(o / jnp.sum(p, axis=-1, keepdims=True))
    o_ref[0] = jnp.concatenate(outs, axis=-1).astype(o_ref.dtype)


def _mem_attention(q, mk, mv, *, tm):
    bsz, seq, width = q.shape
    n_mem = mk.shape[1]
    return pl.pallas_call(
        _mem_attn_kernel,
        grid=(bsz, seq // tm),
        in_specs=[pl.BlockSpec((1, tm, width), lambda b, i: (b, i, 0)),
                  pl.BlockSpec((1, n_mem, width), lambda b, i: (b, 0, 0)),
                  pl.BlockSpec((1, n_mem, width), lambda b, i: (b, 0, 0))],
        out_specs=pl.BlockSpec((1, tm, width), lambda b, i: (b, i, 0)),
        out_shape=jax.ShapeDtypeStruct((bsz, seq, width), BF16),
        compiler_params=_cparams("parallel", "parallel"),
        name="mem_attention",
    )(q, mk, mv)


def _ffn_kernel(*refs, seq, tiles_per_seq):
    (x_ref, g4_ref, g5_ref, wg_ref, wv_ref, cwg_ref, cwv_ref, cbg_ref, cbv_ref, wd_ref) = refs[:10]
    if tiles_per_seq:
        sg_ref, sv_ref, o_ref, og_ref, ov_ref, h_ref, acc_ref, cg_ref, cv_ref = refs[10:]
    else:
        s1g_ref, s1v_ref, s2g_ref, s2v_ref, o_ref, og_ref, ov_ref, h_ref, acc_ref = refs[10:]
    i, j = pl.program_id(0), pl.program_id(1)
    tm = x_ref.shape[0]
    tf = wg_ref.shape[1]

    @pl.when(j == 0)
    def _():
        h_ref[...] = _rms(x_ref[...], g4_ref[...]).astype(BF16)
        acc_ref[...] = jnp.zeros_like(acc_ref)

    h = h_ref[...]

    def conv(u, w_ref, b_ref, prev1, prev2):
        rmod = lax.broadcasted_iota(jnp.int32, (tm, tf), 0) % seq
        s1 = jnp.where(rmod >= 1, pltpu.roll(u, 1, 0), prev1)
        s2 = jnp.where(rmod >= 2, pltpu.roll(u, 2, 0), prev2)
        return s2 * w_ref[0:1, :] + s1 * w_ref[1:2, :] + u * w_ref[2:3, :] + b_ref[...]

    ug = _dot(h, wg_ref[...])
    uv = _dot(h, wv_ref[...])
    if tiles_per_seq:
        @pl.when(i % tiles_per_seq == 0)
        def _():
            cg_ref[j] = sg_ref[0]
            cv_ref[j] = sv_ref[0]

        pg = cg_ref[j]
        pv = cv_ref[j]
        cg_ref[j] = ug[tm - 2:tm, :]
        cv_ref[j] = uv[tm - 2:tm, :]
        og_ref[0] = ug[tm - 2:tm, :]
        ov_ref[0] = uv[tm - 2:tm, :]
        rmod = lax.broadcasted_iota(jnp.int32, (tm, tf), 0)
        cgate = conv(ug, cwg_ref, cbg_ref, pg[1:2, :], jnp.where(rmod == 0, pg[0:1, :], pg[1:2, :]))
        cval = conv(uv, cwv_ref, cbv_ref, pv[1:2, :], jnp.where(rmod == 0, pv[0:1, :], pv[1:2, :]))
    else:
        og_ref[...] = ug
        ov_ref[...] = uv
        cgate = conv(ug, cwg_ref, cbg_ref, s1g_ref[...], s2g_ref[...])
        cval = conv(uv, cwv_ref, cbv_ref, s1v_ref[...], s2v_ref[...])
    act = (cgate * _sigmoid(cgate) * cval).astype(BF16)
    acc_ref[...] += _dot(act, wd_ref[...])

    @pl.when(j == pl.num_programs(1) - 1)
    def _():
        o_ref[...] = x_ref[...] + _rms(acc_ref[...], g5_ref[...])


def _conv_ffn(x, g4, g5, w_up, conv_w, conv_b, w_down, state, *, bsz, seq, tm, tf):
    m, d = x.shape
    d_ff = w_down.shape[0]
    nf = d_ff // tf
    tiles_per_seq = seq // tm if seq % tm == 0 else 0
    kw = FFN_CONV_WIDTH
    in_specs = [pl.BlockSpec((tm, d), lambda i, j: (i, 0)), _full((1, d)), _full((1, d)),
                pl.BlockSpec((d, tf), lambda i, j: (0, j)),
                pl.BlockSpec((d, tf), lambda i, j: (0, nf + j)),
                pl.BlockSpec((kw, tf), lambda i, j: (0, j)),
                pl.BlockSpec((kw, tf), lambda i, j: (0, nf + j)),
                pl.BlockSpec((1, tf), lambda i, j: (0, j)),
                pl.BlockSpec((1, tf), lambda i, j: (0, nf + j)),
                pl.BlockSpec((tf, d), lambda i, j: (j, 0))]
    args = [x, g4, g5, w_up, w_up, conv_w, conv_w, conv_b, conv_b, w_down]
    scratch = [pltpu.VMEM((tm, d), BF16), pltpu.VMEM((tm, d), F32)]
    x_spec = pl.BlockSpec((tm, d), lambda i, j: (i, 0))
    if tiles_per_seq:
        in_specs += [pl.BlockSpec((1, kw - 1, tf), lambda i, j: (i // tiles_per_seq, 0, j)),
                     pl.BlockSpec((1, kw - 1, tf), lambda i, j: (i // tiles_per_seq, 0, nf + j))]
        args += [state, state]
        st_spec = pl.BlockSpec((1, kw - 1, tf), lambda i, j: (i, 0, j))
        out_specs = [x_spec, st_spec, st_spec]
        out_shape = [jax.ShapeDtypeStruct((m, d), F32),
                     jax.ShapeDtypeStruct((m // tm, kw - 1, d_ff), F32),
                     jax.ShapeDtypeStruct((m // tm, kw - 1, d_ff), F32)]
        scratch += [pltpu.VMEM((nf, kw - 1, tf), F32), pltpu.VMEM((nf, kw - 1, tf), F32)]
    else:
        assert tm % seq == 0 and seq >= kw - 1
        pad = seq - (kw - 1)
        s2 = jnp.pad(state, ((0, 0), (0, pad), (0, 0))).reshape(m, 2 * d_ff)
        s1 = jnp.pad(state[:, 1:], ((0, 0), (0, seq - 1), (0, 0))).reshape(m, 2 * d_ff)
        in_specs += [pl.BlockSpec((tm, tf), lambda i, j: (i, j)),
                     pl.BlockSpec((tm, tf), lambda i, j: (i, nf + j)),
                     pl.BlockSpec((tm, tf), lambda i, j: (i, j)),
                     pl.BlockSpec((tm, tf), lambda i, j: (i, nf + j))]
        args += [s1, s1, s2, s2]
        u_spec = pl.BlockSpec((tm, tf), lambda i, j: (i, j))
        out_specs = [x_spec, u_spec, u_spec]
        out_shape = [jax.ShapeDtypeStruct((m, d), F32), jax.ShapeDtypeStruct((m, d_ff), F32),
                     jax.ShapeDtypeStruct((m, d_ff), F32)]
    y, og, ov = pl.pallas_call(
        functools.partial(_ffn_kernel, seq=seq, tiles_per_seq=tiles_per_seq),
        grid=(m // tm, nf),
        in_specs=in_specs,
        out_specs=out_specs,
        out_shape=out_shape,
        scratch_shapes=scratch,
        compiler_params=_cparams("arbitrary", "arbitrary"),
        name="conv_ffn",
    )(*args)
    if tiles_per_seq:
        og = og[tiles_per_seq - 1::tiles_per_seq]
        ov = ov[tiles_per_seq - 1::tiles_per_seq]
    else:
        og = og.reshape(bsz, seq, d_ff)[:, seq - (kw - 1):]
        ov = ov.reshape(bsz, seq, d_ff)[:, seq - (kw - 1):]
    return y, jnp.concatenate([og, ov], axis=-1)


def _block_diag_queries(q, n_heads):
    t, width = q.shape
    tiled = jnp.concatenate([q.astype(F32)] * n_heads, axis=0)
    row = lax.broadcasted_iota(jnp.int32, (n_heads * t, width), 0)
    col = lax.broadcasted_iota(jnp.int32, (n_heads * t, width), 1)
    return jnp.where(row // t == col // HEAD_DIM, tiled, 0.0).astype(BF16)


def _diag_blocks(acc, t, n_heads):
    return jnp.concatenate(
        [acc[h * t:(h + 1) * t, h * HEAD_DIM:(h + 1) * HEAD_DIM] for h in range(n_heads)], axis=-1)


def _pool_key_norm_kernel(k_ref, o_ref, *, n_heads):
    @pl.when(pl.program_id(0) == 0)
    def _():
        o_ref[...] = jnp.zeros_like(o_ref)

    pages, width, page = k_ref.shape
    row = lax.broadcasted_iota(jnp.int32, (n_heads, width), 0)
    col = lax.broadcasted_iota(jnp.int32, (n_heads, width), 1)
    pick = jnp.where(col // (width // n_heads) == row, 1.0, 0.0).astype(BF16)
    top = o_ref[...]
    for p_ in range(pages):
        top = jnp.maximum(top, _dot(pick, jnp.square(k_ref[p_]).astype(BF16)))
    o_ref[...] = top


def _pool_key_norms(k_pool, n_heads):
    n_pool, width, page = k_pool.shape
    pages = max(p_ for p_ in (16, 8, 4, 2, 1) if n_pool % p_ == 0)
    return pl.pallas_call(
        functools.partial(_pool_key_norm_kernel, n_heads=n_heads),
        grid=(n_pool // pages,),
        in_specs=[pl.BlockSpec((pages, width, page), lambda i: (i, 0, 0))],
        out_specs=pl.BlockSpec((n_heads, page), lambda i: (0, 0)),
        out_shape=jax.ShapeDtypeStruct((n_heads, page), F32),
        compiler_params=_cparams("arbitrary"),
        name="pool_key_norms",
    )(k_pool)


def _fox_decode_kernel(pt_ref, q_ref, kn_ref, vn_ref, lq_ref, kmax_ref, k_pool, v_pool, lf_pool, o_ref,
                       k_buf, v_buf, lf_buf, sems, qbd_ref, m_ref, l_ref, acc_ref, carry_ref, gq_ref,
                       *, n_pages, n_heads, t_new):
    b = pl.program_id(0)
    rows = n_heads * t_new
    later = _later_key_matrix(LANES, F32)
    lane = lax.broadcasted_iota(jnp.int32, (rows, LANES), 1)
    qpos = lax.broadcasted_iota(jnp.int32, (rows, LANES), 0) % t_new

    def page_copies(p, slot):
        page = pt_ref[b, n_pages - 1 - p]
        return (pltpu.make_async_copy(k_pool.at[page], k_buf.at[slot], sems.at[0, slot]),
                pltpu.make_async_copy(v_pool.at[page], v_buf.at[slot], sems.at[1, slot]),
                pltpu.make_async_copy(lf_pool.at[page], lf_buf.at[slot], sems.at[2, slot]))

    for cp in page_copies(0, 0):
        cp.start()

    def rows_of_heads(per_head):
        n = per_head.shape[1]
        return jnp.concatenate(
            [jnp.broadcast_to(per_head[h:h + 1, :], (t_new, n)) for h in range(n_heads)], axis=0)

    def update(s, v):
        m_prev = m_ref[...]
        m_new = jnp.maximum(m_prev, jnp.max(s, axis=-1, keepdims=True))
        p = jnp.exp(s - m_new)
        alpha = jnp.exp(m_prev - m_new)
        l_ref[...] = alpha * l_ref[...] + jnp.sum(p, axis=-1, keepdims=True)
        acc_ref[...] = alpha * acc_ref[...] + _dot_nt(p.astype(BF16), v)
        m_ref[...] = m_new

    qbd_ref[...] = _block_diag_queries(q_ref[0], n_heads)
    m_ref[...] = jnp.full_like(m_ref, NEG)
    l_ref[...] = jnp.zeros_like(l_ref)
    acc_ref[...] = jnp.zeros_like(acc_ref)
    logf_rows = lq_ref[0]
    gap = _dot_exact(logf_rows, later)
    carry_ref[...] = gap[:, 0:1] + logf_rows[:, 0:1]
    gq_ref[...] = jnp.sum(jnp.where(lane == qpos, gap, 0.0), axis=-1, keepdims=True)
    s = (gap - gq_ref[...]) + _dot(qbd_ref[...], kn_ref[0])
    update(jnp.where(lane <= qpos, s, NEG), vn_ref[0])

    q32 = qbd_ref[...].astype(F32)
    k2_max = rows_of_heads(jnp.max(kmax_ref[...], axis=-1, keepdims=True))
    cap = jnp.sqrt(jnp.sum(q32 * q32, axis=-1, keepdims=True) * k2_max) * 1.01

    def headroom():
        return jnp.max(cap + carry_ref[...] - gq_ref[...] - m_ref[...])

    def more(carry):
        p, top = carry
        return jnp.logical_and(p < n_pages, top > SB_DEAD)

    def body(carry):
        p, _ = carry
        slot = p % 2
        for cp in page_copies(p, slot):
            cp.wait()

        @pl.when(p + 1 < n_pages)
        def _():
            for cp in page_copies(p + 1, 1 - slot):
                cp.start()

        logf_rows = rows_of_heads(lf_buf[slot])
        gap = _dot_exact(logf_rows, later) + carry_ref[...]
        carry_ref[...] = gap[:, 0:1] + logf_rows[:, 0:1]
        s = (gap - gq_ref[...]) + _dot(qbd_ref[...], k_buf[slot].astype(BF16))
        update(s, v_buf[slot].astype(BF16))
        return p + 1, headroom()

    p_end, _ = lax.while_loop(more, body, (jnp.int32(0), headroom()))

    @pl.when(p_end < n_pages)
    def _():
        for cp in page_copies(p_end, p_end % 2):
            cp.wait()

    o_ref[0] = _diag_blocks(acc_ref[...] / l_ref[...], t_new, n_heads).astype(o_ref.dtype)


def _fox_decode_attention(page_table, q, k_new, v_new, k_pool, v_pool, logf_new, logf_pool):
    bsz, t_new, width = q.shape
    n_heads = width // HEAD_DIM
    n_pages = page_table.shape[1]
    page = k_pool.shape[2]
    assert page == LANES and t_new <= page
    rows = n_heads * t_new
    pad = ((0, 0), (0, 0), (0, page - t_new))
    kn, vn = jnp.pad(jnp.swapaxes(k_new, 1, 2), pad), jnp.pad(jnp.swapaxes(v_new, 1, 2), pad)
    lq = jnp.pad(jnp.repeat(jnp.swapaxes(logf_new, 1, 2), t_new, axis=1), pad)
    lf_t = jnp.swapaxes(logf_pool, 1, 2)
    per_seq = lambda shape: pl.BlockSpec(shape, lambda b, pt: (b, 0, 0))
    anywhere = pl.BlockSpec(memory_space=pl.ANY)
    return pl.pallas_call(
        functools.partial(_fox_decode_kernel, n_pages=n_pages, n_heads=n_heads, t_new=t_new),
        grid_spec=pltpu.PrefetchScalarGridSpec(
            num_scalar_prefetch=1,
            grid=(bsz,),
            in_specs=[per_seq((1, t_new, width)), per_seq((1, width, page)), per_seq((1, width, page)),
                      per_seq((1, rows, page)), pl.BlockSpec((n_heads, page), lambda b, pt: (0, 0)),
                      anywhere, anywhere, anywhere],
            out_specs=per_seq((1, t_new, width)),
            scratch_shapes=[pltpu.VMEM((2, width, page), F32), pltpu.VMEM((2, width, page), F32),
                            pltpu.VMEM((2, n_heads, page), F32), pltpu.SemaphoreType.DMA((3, 2)),
                            pltpu.VMEM((rows, width), BF16), pltpu.VMEM((rows, 1), F32),
                            pltpu.VMEM((rows, 1), F32), pltpu.VMEM((rows, width), F32),
                            pltpu.VMEM((rows, 1), F32), pltpu.VMEM((rows, 1), F32)]),
        out_shape=jax.ShapeDtypeStruct((bsz, t_new, width), F32),
        compiler_params=_cparams("arbitrary"),
        name="fox_decode_attention",
    )(page_table, q, kn, vn, lq, _pool_key_norms(k_pool, n_heads), k_pool, v_pool, lf_t)


def _sb_decode_kernel(pt_ref, q_ref, kn_ref, vn_ref, k_pool, v_pool, o_ref,
                      k_buf, v_buf, sems, qbd_ref, r_ref, acc_ref, *, n_pages, n_heads, t_new):
    b = pl.program_id(0)
    rows = n_heads * t_new
    later = _later_key_matrix(LANES, BF16)

    def page_copies(p, slot):
        page = pt_ref[b, n_pages - 1 - p]
        return (pltpu.make_async_copy(k_pool.at[page], k_buf.at[slot], sems.at[0, slot]),
                pltpu.make_async_copy(v_pool.at[page], v_buf.at[slot], sems.at[1, slot]))

    for cp in page_copies(0, 0):
        cp.start()

    qbd_ref[...] = _block_diag_queries(q_ref[0], n_heads)
    lane = lax.broadcasted_iota(jnp.int32, (rows, LANES), 1)
    qpos = lax.broadcasted_iota(jnp.int32, (rows, LANES), 0) % t_new
    o, r = _sb_block(qbd_ref[...], kn_ref[0], vn_ref[0], later, 0.0, lane < qpos, transposed=True)
    acc_ref[...] = o
    r_ref[...] = r

    def more(carry):
        p, r_top = carry
        return jnp.logical_and(p < n_pages, r_top > SB_DEAD)

    def body(carry):
        p, _ = carry
        slot = p % 2
        for cp in page_copies(p, slot):
            cp.wait()

        @pl.when(p + 1 < n_pages)
        def _():
            for cp in page_copies(p + 1, 1 - slot):
                cp.start()

        o, r = _sb_block(qbd_ref[...], k_buf[slot].astype(BF16), v_buf[slot].astype(BF16),
                         later, r_ref[...], None, transposed=True)
        acc_ref[...] += o
        r_ref[...] = r
        return p + 1, jnp.max(r)

    p_end, _ = lax.while_loop(more, body, (jnp.int32(0), jnp.max(r_ref[...])))

    @pl.when(p_end < n_pages)
    def _():
        for cp in page_copies(p_end, p_end % 2):
            cp.wait()

    o_ref[0] = _diag_blocks(acc_ref[...], t_new, n_heads).astype(o_ref.dtype)


def _sb_decode_attention(page_table, q, k_new, v_new, k_pool, v_pool):
    bsz, t_new, width = q.shape
    n_heads = width // HEAD_DIM
    n_pages = page_table.shape[1]
    page = k_pool.shape[2]
    assert page == LANES and t_new <= page
    rows = n_heads * t_new
    pad = ((0, 0), (0, 0), (0, page - t_new))
    kn, vn = jnp.pad(jnp.swapaxes(k_new, 1, 2), pad), jnp.pad(jnp.swapaxes(v_new, 1, 2), pad)
    per_seq = lambda shape: pl.BlockSpec(shape, lambda b, pt: (b, 0, 0))
    return pl.pallas_call(
        functools.partial(_sb_decode_kernel, n_pages=n_pages, n_heads=n_heads, t_new=t_new),
        grid_spec=pltpu.PrefetchScalarGridSpec(
            num_scalar_prefetch=1,
            grid=(bsz,),
            in_specs=[per_seq((1, t_new, width)), per_seq((1, width, page)), per_seq((1, width, page)),
                      pl.BlockSpec(memory_space=pl.ANY), pl.BlockSpec(memory_space=pl.ANY)],
            out_specs=per_seq((1, t_new, width)),
            scratch_shapes=[pltpu.VMEM((2, width, page), F32), pltpu.VMEM((2, width, page), F32),
                            pltpu.SemaphoreType.DMA((2, 2)),
                            pltpu.VMEM((rows, width), BF16), pltpu.VMEM((rows, 1), F32),
                            pltpu.VMEM((rows, width), F32)]),
        out_shape=jax.ShapeDtypeStruct((bsz, t_new, width), F32),
        compiler_params=_cparams("arbitrary"),
        name="sb_decode_attention",
    )(page_table, q, kn, vn, k_pool, v_pool)


def _prep_weights(p):
    w = {}
    n_even = p["even_w_in"].shape[0]
    ch = p["conv_w"].shape[2]
    even_in = p["even_w_in"].shape[2]
    n_fox = even_in - 5 * ch
    w["even_w_in"] = jnp.pad(p["even_w_in"], ((0, 0), (0, 0), (0, LANES - n_fox))).astype(BF16)
    w["even_b_f"] = jnp.pad(p["even_b_f"], ((0, 0), (0, LANES - n_fox))).reshape(n_even, 1, LANES)
    w["conv_w"] = jnp.pad(p["conv_w"], ((0, 0), (0, CONV_HALO - CONV_WIDTH), (0, 0)))
    for name in ("even_w_out", "sb_w_in", "sb_w_out", "mem_wq", "mem_wo", "ffn_w_up", "ffn_w_down"):
        w[name] = p[name].astype(BF16)
    w["mem_wkv"] = jnp.concatenate([p["mem_wk"], p["mem_wv"]], axis=-1).astype(BF16)
    return w


def _pages_transposed(pool):
    n, page, heads, dim = pool.shape
    return jnp.transpose(pool, (0, 2, 3, 1)).reshape(n, heads * dim, page)


def _run_trunk(x, mem_k, mem_v, conv_state, ffn_state, fox_cache, sb_cache, page_table, p, w):
    bsz, seq, d = x.shape
    m = bsz * seq
    depth = p["norm_g"].shape[0]
    ch = p["conv_w"].shape[2]
    n_fox = p["even_b_f"].shape[1]
    prompt = fox_cache is None
    tm = min(m, 256)
    tq = min(seq, 256)
    x = x.reshape(m, d)
    fox_k, fox_v, fox_logf, conv_new, sb_k, sb_v, ffn_new = [], [], [], [], [], [], []
    for layer in range(depth):
        g = p["norm_g"][layer].reshape(6, 1, d)
        i = layer // 2
        if layer % 2 == 0:
            u, q, k, v, kb, vb, logf, cum = _even_proj(
                x, g[0], w["even_w_in"][i], w["even_b_f"][i], ch=ch, nh=n_fox, seq=seq, tm=tm)
            u3 = u.reshape(bsz, seq, ch)
            state = conv_state[i]
            halo = CONV_WIDTH - 1
            state_pad = jnp.pad(state, ((0, 0), (CONV_HALO - halo, 0), (0, 0)))
            a_out = _conv_module(u3, state_pad, w["conv_w"][i], p["conv_b"][i].reshape(1, ch),
                                 p["conv_ln_g"][i].reshape(1, ch), p["conv_ln_b"][i].reshape(1, ch))
            conv_new.append(jnp.concatenate([state, u3], axis=1)[:, seq:])
            q3, kb3, vb3 = (a.reshape(bsz, seq, ch) for a in (q, kb, vb))
            if prompt:
                c4 = cum.reshape(bsz, seq, n_fox // 2, 2)
                o = _fox_attention(q3, kb3, vb3, jnp.transpose(c4, (0, 2, 1, 3)),
                                   jnp.transpose(c4, (0, 2, 3, 1)), tq=tq)
            else:
                pool = lambda c: _pages_transposed(c[i])
                o = _fox_decode_attention(page_table, q3.astype(F32), kb3, vb3, pool(fox_cache[0]),
                                          pool(fox_cache[1]), logf.reshape(bsz, seq, n_fox), fox_cache[2][i])
            x = _linear_residual([a_out.reshape(m, ch), o.reshape(m, ch).astype(BF16)],
                                 [w["even_w_out"][i][:ch], w["even_w_out"][i][ch:]], x, g[1], tm=tm)
            fox_k.append(k.reshape(bsz, seq, n_fox, HEAD_DIM))
            fox_v.append(v.reshape(bsz, seq, n_fox, HEAD_DIM))
            fox_logf.append(logf.reshape(bsz, seq, n_fox))
        else:
            q, k, v, kb, vb = _odd_proj(x, g[0], w["sb_w_in"][i], tm=tm)
            width = q.shape[1]
            q3, kb3, vb3 = (a.reshape(bsz, seq, width) for a in (q, kb, vb))
            if prompt:
                o = _sb_attention(q3, kb3, vb3, tq=tq)
            else:
                pool = lambda c: _pages_transposed(c[i])
                o = _sb_decode_attention(page_table, q3.astype(F32), kb3, vb3, pool(sb_cache[0]), pool(sb_cache[1]))
            x = _linear_residual([o.reshape(m, width).astype(BF16)], [w["sb_w_out"][i]], x, g[1], tm=tm)
            sb_k.append(k.reshape(bsz, seq, width // HEAD_DIM, HEAD_DIM))
            sb_v.append(v.reshape(bsz, seq, width // HEAD_DIM, HEAD_DIM))
        qm = _norm_matmul(x, g[2], w["mem_wq"][layer], tm=tm, out_dtype=BF16 if seq % 16 == 0 else F32,
                          scale=MEM_HEAD_DIM ** -0.5)
        mw = qm.shape[1]
        om = _mem_attention(qm.reshape(bsz, seq, mw), mem_k[layer], mem_v[layer], tm=min(seq, 512))
        x = _linear_residual([om.reshape(m, mw)], [w["mem_wo"][layer]], x, g[3], tm=tm)
        d_ff2 = p["ffn_w_up"].shape[2]
        x, buf = _conv_ffn(x, g[4], g[5], w["ffn_w_up"][layer], p["ffn_conv_w"][layer],
                           p["ffn_conv_b"][layer].reshape(1, d_ff2), w["ffn_w_down"][layer],
                           ffn_state[layer], bsz=bsz, seq=seq, tm=min(m, 1024), tf=256)
        ffn_new.append(buf)
    return (x.reshape(bsz, seq, d), jnp.stack(fox_k), jnp.stack(fox_v), jnp.stack(fox_logf),
            jnp.stack(conv_new), jnp.stack(sb_k), jnp.stack(sb_v), jnp.stack(ffn_new))


def kernel(x_prompt, x_sample, cache_fox_k, cache_fox_v, cache_fox_logf, state_conv, cache_sb_k, cache_sb_v, cache_mem_k, cache_mem_v, state_ffn_conv, page_table, mem_prompt, norm_g, even_w_in, even_b_f, conv_w, conv_b, conv_ln_g, conv_ln_b, even_w_out, sb_w_in, sb_w_out, mem_norm_g, mem_wq, mem_wk, mem_wv, mem_wo, ffn_w_up, ffn_conv_w, ffn_conv_b, ffn_w_down):
    p = dict(norm_g=norm_g, even_w_in=even_w_in, even_b_f=even_b_f, conv_w=conv_w, conv_b=conv_b,
             conv_ln_g=conv_ln_g, conv_ln_b=conv_ln_b, even_w_out=even_w_out, sb_w_in=sb_w_in,
             sb_w_out=sb_w_out, mem_wq=mem_wq, mem_wk=mem_wk, mem_wv=mem_wv, mem_wo=mem_wo,
             ffn_w_up=ffn_w_up, ffn_conv_w=ffn_conv_w, ffn_conv_b=ffn_conv_b, ffn_w_down=ffn_w_down)
    w = _prep_weights(p)
    depth, d = norm_g.shape[0], norm_g.shape[2]
    bsz, n_mem = mem_prompt.shape[0], mem_prompt.shape[1]
    mem_width = mem_wq.shape[2]
    n_even = even_w_in.shape[0]
    ch = conv_w.shape[2]

    mem_flat = mem_prompt.reshape(bsz * n_mem, d)
    mem_kv = [_norm_matmul(mem_flat, mem_norm_g[l].reshape(1, d), w["mem_wkv"][l], tm=256, out_dtype=F32)
              for l in range(depth)]
    mem_k_prompt = jnp.stack([kv[:, :mem_width].reshape(bsz, n_mem, mem_width) for kv in mem_kv])
    mem_v_prompt = jnp.stack([kv[:, mem_width:].reshape(bsz, n_mem, mem_width) for kv in mem_kv])
    conv_zero = jnp.zeros((n_even, bsz, CONV_WIDTH - 1, ch), F32)
    ffn_zero = jnp.zeros((depth, bsz, FFN_CONV_WIDTH - 1, ffn_w_up.shape[2]), F32)
    (y_prompt, fox_k_prompt, fox_v_prompt, fox_logf_prompt, conv_state_prompt,
     sb_k_prompt, sb_v_prompt, ffn_state_prompt) = _run_trunk(
        x_prompt, mem_k_prompt, mem_v_prompt, conv_zero, ffn_zero, None, None, None, p, w)

    dbsz = x_sample.shape[0]
    cmk = cache_mem_k.reshape(depth, dbsz, n_mem, mem_width)
    cmv = cache_mem_v.reshape(depth, dbsz, n_mem, mem_width)
    (y_sample, fox_k_sample, fox_v_sample, fox_logf_sample, conv_state_sample,
     sb_k_sample, sb_v_sample, ffn_state_sample) = _run_trunk(
        x_sample, cmk, cmv, state_conv, state_ffn_conv,
        (cache_fox_k, cache_fox_v, cache_fox_logf), (cache_sb_k, cache_sb_v), page_table, p, w)

    mem_shape = (depth, bsz, n_mem, MEM_HEADS, MEM_HEAD_DIM)
    return (y_prompt, y_sample, fox_k_prompt, fox_v_prompt, fox_logf_prompt, conv_state_prompt,
            sb_k_prompt, sb_v_prompt, mem_k_prompt.reshape(mem_shape), mem_v_prompt.reshape(mem_shape),
            ffn_state_prompt, fox_k_sample, fox_v_sample, fox_logf_sample, conv_state_sample,
            sb_k_sample, sb_v_sample, ffn_state_sample)
```
